```python
import math
import jax, jax.numpy as jnp
from jax import lax
import numpy as np

D_MODEL = 1024
BATCH = 2
SEQ = 8192
DEPTH = 4

CHUNK = 64
Q_BLOCK = 128
ROPE_THETA = 10000.0
EPS = 1e-6
D_FF = 2816
N_BRANCH = 4
MIX_WIDTH = 512

DA_HEADS = 4
DA_QK_DIM = 64
DA_V_DIM = 128
GLA_HEADS = 4
GLA_DK = 64
GLA_DV = 128
GLA_GATE_RANK = 16
GLA_GATE_TAU = 16.0
SSD_HEADS = 8
SSD_HEAD_DIM = 64
SSD_GROUPS = 2
SSD_STATE = 128
SSD_CONV = 4
SSD_INNER = SSD_HEADS * SSD_HEAD_DIM
SSD_CONV_DIM = SSD_INNER + 2 * SSD_GROUPS * SSD_STATE
RW_HEADS = 8
RW_HEAD = 64
RW_DIM = RW_HEADS * RW_HEAD
RW_DECAY_RANK = 64
RW_A_RANK = 64
RW_GATE_RANK = 128
RW_DECAY_SCALE = 0.606531
RW_GN_EPS = 64e-5

DA_SIZES = (DA_HEADS * 2 * DA_QK_DIM, DA_HEADS * 2 * DA_QK_DIM, DA_HEADS * DA_V_DIM)
GLA_SIZES = (GLA_HEADS * GLA_DK, GLA_HEADS * GLA_DK, GLA_HEADS * GLA_DV, GLA_HEADS * GLA_DV, GLA_GATE_RANK)
SSD_SIZES = (SSD_INNER, SSD_CONV_DIM, SSD_HEADS)
RW_SIZES = (RW_DIM, RW_DIM, RW_DIM, RW_DECAY_RANK, RW_A_RANK, RW_GATE_RANK)
GATE_COLS = N_BRANCH * D_MODEL
MIXER_COLS = (sum(DA_SIZES), sum(GLA_SIZES), sum(SSD_SIZES), sum(RW_SIZES), GATE_COLS)
N_IN = sum(MIXER_COLS)

kernel_name = "hybrid_gated_diffattn_gla_ssd_rwkv7_macaron"


def rmsnorm(x, g, eps=EPS):
    xf = x.astype(jnp.float32)
    y = xf * lax.rsqrt(jnp.mean(jnp.square(xf), axis=-1, keepdims=True) + eps)
    return (y * g.astype(jnp.float32)).astype(x.dtype)


def swiglu_ffn(x, wg, wu, wd):
    return (jax.nn.silu(x @ wg) * (x @ wu)) @ wd


def split_cols(t, sizes):
    return jnp.split(t, np.cumsum(sizes)[:-1].tolist(), axis=-1)


def token_shift(t):
    return jnp.pad(t, ((0, 0), (1, 0), (0, 0)))[:, :-1]


def to_chunks(t):
    B, S, H = t.shape[:3]
    t = t.reshape((B, S // CHUNK, CHUNK, H) + t.shape[3:])
    return jnp.moveaxis(t, (1, 3), (0, 2))


def from_chunks(t):
    n, B, H, C, d = t.shape
    return jnp.moveaxis(t, (0, 2), (1, 3)).reshape(B, n * C, H, d)


def rope(t):
    S, d = t.shape[1], t.shape[-1]
    half = d // 2
    inv_freq = ROPE_THETA ** (-jnp.arange(half, dtype=jnp.float32) / half)
    ang = jnp.arange(S, dtype=jnp.float32)[:, None] * inv_freq[None, :]
    cos = jnp.cos(ang)[None, :, None, :]
    sin = jnp.sin(ang)[None, :, None, :]
    tf = t.astype(jnp.float32)
    t1, t2 = tf[..., :half], tf[..., half:]
    return jnp.concatenate([t1 * cos - t2 * sin, t2 * cos + t1 * sin], axis=-1).astype(t.dtype)


def causal_depthwise_conv(x, w, b):
    K, C = w.shape
    xp = jnp.pad(x, ((0, 0), (K - 1, 0), (0, 0)))
    y = lax.conv_general_dilated(xp, w[:, None, :].astype(x.dtype), window_strides=(1,), padding='VALID',
                                 dimension_numbers=('NWC', 'WIO', 'NWC'), feature_group_count=C)
    return y + b


def differential_attention(q, k, v, lam, lam_init, norm_g):
    B, S, H, _, dq = q.shape
    q = rope(q.reshape(B, S, H * 2, dq)).reshape(B, S, H, 2, dq) * (dq ** -0.5)
    k = rope(k.reshape(B, S, H * 2, dq)).reshape(B, S, H, 2, dq)
    n_blk = S // Q_BLOCK
    q_blocks = jnp.moveaxis(q.reshape(B, n_blk, Q_BLOCK, H, 2, dq), 1, 0)
    key_chunk = jnp.arange(S) // CHUNK

    def attend(args):
        q_blk, blk = args
        query_chunk = (blk * Q_BLOCK + jnp.arange(Q_BLOCK)) // CHUNK
        allowed = key_chunk[None, :] <= query_chunk[:, None]
        s = jnp.einsum('bqhcd,bkhcd->bhcqk', q_blk, k).astype(jnp.float32)
        p = jax.nn.softmax(jnp.where(allowed, s, -jnp.inf), axis=-1)
        a = p[:, :, 0] - lam * p[:, :, 1]
        return jnp.einsum('bhqk,bkhd->bqhd', a.astype(v.dtype), v)

    o = lax.map(attend, (q_blocks, jnp.arange(n_blk)))
    o = jnp.moveaxis(o, 0, 1).reshape(B, S, H, -1)
    o = rmsnorm(o, norm_g) * (1.0 - lam_init)
    return o.reshape(B, S, -1)


def gla_chunked(q, k, v, log_g):
    B, S, H, dk = q.shape
    dv = v.shape[-1]
    f32 = jnp.float32
    qc, kc, vc, gc = (to_chunks(t.astype(f32)) for t in (q, k, v, log_g))
    G = jnp.cumsum(gc, axis=3)
    G_ref = G[:, :, :, CHUNK // 2:CHUNK // 2 + 1]
    causal = jnp.tril(jnp.ones((CHUNK, CHUNK), bool))
    att = jnp.einsum('cbhik,cbhjk->cbhij', qc * jnp.exp(G - G_ref), kc * jnp.exp(G_ref - G))
    att = jnp.where(causal, att, 0.0)
    y_intra = jnp.einsum('cbhij,cbhjv->cbhiv', att, vc)
    G_last = G[:, :, :, -1]
    chunk_kv = jnp.einsum('cbhjk,cbhjv->cbhkv', kc * jnp.exp(G_last[:, :, :, None] - G), vc)
    q_in = qc * jnp.exp(G)

    def step(state, inp):
        dec, kv, qi = inp
        y = jnp.einsum('bhik,bhkv->bhiv', qi, state)
        return jnp.exp(dec)[..., None] * state + kv, y

    _, y_inter = lax.scan(step, jnp.zeros((B, H, dk, dv), f32), (G_last, chunk_kv, q_in))
    return from_chunks(y_intra + y_inter)


def ssd_chunked(x, dt, A, bm, cm):
    B, S, H, P = x.shape
    N = bm.shape[-1]
    f32 = jnp.float32
    xc = to_chunks(x.astype(f32) * dt[..., None])
    bc = to_chunks(bm.astype(f32))
    cc = to_chunks(cm.astype(f32))
    acum = jnp.cumsum(to_chunks(dt * A), axis=-1)
    causal = jnp.tril(jnp.ones((CHUNK, CHUNK), bool))
    L = jnp.exp(jnp.where(causal, acum[..., :, None] - acum[..., None, :], -jnp.inf))
    scores = jnp.einsum('cbhin,cbhjn->cbhij', cc, bc) * L
    y_intra = jnp.einsum('cbhij,cbhjp->cbhip', scores, xc)
    decay_to_end = jnp.exp(acum[..., -1:] - acum)
    chunk_states = jnp.einsum('cbhjn,cbhj,cbhjp->cbhpn', bc, decay_to_end, xc)
    c_in = cc * jnp.exp(acum)[..., None]
    chunk_decay = jnp.exp(acum[..., -1])

    def step(h, inp):
        dec, st, ci = inp
        y = jnp.einsum('bhin,bhpn->bhip', ci, h)
        return dec[..., None, None] * h + st, y

    _, y_inter = lax.scan(step, jnp.zeros((B, H, P, N), f32), (chunk_decay, chunk_states, c_in))
    return from_chunks(y_intra + y_inter)


def rwkv7_recurrence(r, w, k, v, kk, a):
    B, S, H, N = r.shape

    def step(state, inp):
        r_t, w_t, k_t, v_t, kk_t, a_t = inp
        sa = jnp.einsum('bhvk,bhk->bhv', state, -kk_t)
        state = (state * w_t[:, :, None, :] + sa[..., None] * (kk_t * a_t)[:, :, None, :]
                 + v_t[..., None] * k_t[:, :, None, :])
        return state, jnp.einsum('bhvk,bhk->bhv', state, r_t)

    xs = tuple(jnp.moveaxis(t.astype(jnp.float32), 1, 0) for t in (r, w, k, v, kk, a))
    _, y = lax.scan(step, jnp.zeros((B, H, N, N), jnp.float32), xs)
    return jnp.moveaxis(y, 0, 1)


def hybrid_token_mixer(u, lidx, w_in, da_lambda_q1, da_lambda_k1, da_lambda_q2, da_lambda_k2, da_norm,
                       gla_gate_w2, gla_gate_b, gla_norm,
                       ssd_conv_w, ssd_conv_b, ssd_dt_bias, ssd_a_log, ssd_d, ssd_norm,
                       rw_mu, rw_w0, rw_w2, rw_a0, rw_a2, rw_g2, rw_k_k, rw_k_a, rw_r_k, rw_norm_w, rw_norm_b,
                       w_branch, gate_b, w_out):
    B, S, _ = u.shape
    f32 = jnp.float32
    p = u @ w_in
    p_da, p_gla, p_ssd, p_rw, p_gate = split_cols(p, MIXER_COLS)

    q, k, v = split_cols(p_da, DA_SIZES)
    lam_init = 0.8 - 0.6 * math.exp(-0.3 * lidx)
    lam = (jnp.exp(jnp.sum(da_lambda_q1.astype(f32) * da_lambda_k1.astype(f32)))
           - jnp.exp(jnp.sum(da_lambda_q2.astype(f32) * da_lambda_k2.astype(f32))) + lam_init)
    y_a = differential_attention(q.reshape(B, S, DA_HEADS, 2, DA_QK_DIM), k.reshape(B, S, DA_HEADS, 2, DA_QK_DIM),
                                 v.reshape(B, S, DA_HEADS, DA_V_DIM), lam, lam_init, da_norm)

    q, k, v, og, glr = split_cols(p_gla, GLA_SIZES)
    log_g = jax.nn.log_sigmoid((glr @ gla_gate_w2 + gla_gate_b).astype(f32)) / GLA_GATE_TAU
    o = gla_chunked(q.reshape(B, S, GLA_HEADS, GLA_DK) * (GLA_DK ** -0.5), k.reshape(B, S, GLA_HEADS, GLA_DK),
                    v.reshape(B, S, GLA_HEADS, GLA_DV), log_g.reshape(B, S, GLA_HEADS, GLA_DK))
    y_b = rmsnorm(o, gla_norm).reshape(B, S, -1).astype(u.dtype) * jax.nn.silu(og)

    z, xbc, dt = split_cols(p_ssd, SSD_SIZES)
    xbc = jax.nn.silu(causal_depthwise_conv(xbc, ssd_conv_w, ssd_conv_b))
    xs, bm, cm = split_cols(xbc, (SSD_INNER, SSD_GROUPS * SSD_STATE, SSD_GROUPS * SSD_STATE))
    dt = jax.nn.softplus((dt + ssd_dt_bias).astype(f32))
    A = -jnp.exp(ssd_a_log.astype(f32))
    rep = SSD_HEADS // SSD_GROUPS
    bm = jnp.repeat(bm.reshape(B, S, SSD_GROUPS, SSD_STATE), rep, axis=2)
    cm = jnp.repeat(cm.reshape(B, S, SSD_GROUPS, SSD_STATE), rep, axis=2)
    xh = xs.reshape(B, S, SSD_HEADS, SSD_HEAD_DIM)
    y = ssd_chunked(xh, dt, A, bm, cm) + ssd_d.astype(f32)[:, None] * xh.astype(f32)
    y = y.reshape(B, S, SSD_INNER) * jax.nn.silu(z.astype(f32))
    y_c = rmsnorm(y.reshape(B, S, SSD_GROUPS, -1), ssd_norm.reshape(SSD_GROUPS, -1)).reshape(B, S, SSD_INNER)
    y_c = y_c.astype(u.dtype)

    p_rw = p_rw + (token_shift(p_rw) - p_rw) * rw_mu
    r, k, v, wlr, alr, glr = split_cols(p_rw, RW_SIZES)
    log_w = -RW_DECAY_SCALE * jax.nn.sigmoid((rw_w0 + jnp.tanh(wlr) @ rw_w2).astype(f32))
    a = jax.nn.sigmoid((rw_a0 + alr @ rw_a2).astype(f32))
    g = jax.nn.sigmoid(glr) @ rw_g2
    hs = (B, S, RW_HEADS, RW_HEAD)
    kk = (k.astype(f32) * rw_k_k.astype(f32)).reshape(hs)
    kk = kk / jnp.maximum(jnp.sqrt(jnp.sum(kk * kk, axis=-1, keepdims=True)), 1e-12)
    k = k.astype(f32) * (1.0 + (a - 1.0) * rw_k_a.astype(f32))
    r4, k4, v4, a4 = r.astype(f32).reshape(hs), k.reshape(hs), v.astype(f32).reshape(hs), a.reshape(hs)
    y = rwkv7_recurrence(r4, jnp.exp(log_w).reshape(hs), k4, v4, kk, a4)
    mu = jnp.mean(y, axis=-1, keepdims=True)
    var = jnp.mean(jnp.square(y - mu), axis=-1, keepdims=True)
    y = ((y - mu) * lax.rsqrt(var + RW_GN_EPS)).reshape(B, S, RW_DIM) * rw_norm_w.astype(f32) + rw_norm_b.astype(f32)
    bonus = jnp.sum(r4 * k4 * rw_r_k.astype(f32).reshape(RW_HEADS, RW_HEAD), axis=-1, keepdims=True) * v4
    y_d = ((y + bonus.reshape(B, S, RW_DIM)) * g.astype(f32)).astype(u.dtype)

    ys = jnp.stack([y_a.astype(u.dtype), y_b.astype(u.dtype), y_c, y_d], axis=2)
    proj = jnp.einsum('bsnc,ncd->bsnd', ys, w_branch)
    gates = jax.nn.sigmoid(p_gate.reshape(B, S, N_BRANCH, D_MODEL) + gate_b)
    merged = jnp.sum(gates * proj, axis=2)
    return merged @ w_out


def setup_inputs(seed: int = 0) -> dict:
    key = jax.random.key(seed)
    ks = iter(jax.random.split(key, 64))
    L, D, F = DEPTH, D_MODEL, D_FF

    def nrm(shape, scale):
        return jax.random.normal(next(ks), shape, jnp.float32) * scale

    def gain(shape):
        return 1.0 + nrm(shape, 0.01)

    x = nrm((BATCH, SEQ, D), 1.0)
    ffn1_norm = gain((L, D))
    ffn1_wg = nrm((L, D, F), D ** -0.5)
    ffn1_wu = nrm((L, D, F), D ** -0.5)
    ffn1_wd = nrm((L, F, D), F ** -0.5)
    mix_norm = gain((L, D))
    w_in = nrm((L, D, N_IN), D ** -0.5)
    da_lambda_q1 = nrm((L, DA_QK_DIM), 0.1)
    da_lambda_k1 = nrm((L, DA_QK_DIM), 0.1)
    da_lambda_q2 = nrm((L, DA_QK_DIM), 0.1)
    da_lambda_k2 = nrm((L, DA_QK_DIM), 0.1)
    da_norm = gain((L, DA_V_DIM))
    gla_gate_w2 = nrm((L, GLA_GATE_RANK, GLA_HEADS * GLA_DK), GLA_GATE_RANK ** -0.5)
    gla_gate_b = nrm((L, GLA_HEADS * GLA_DK), 0.1)
    gla_norm = gain((L, GLA_DV))
    ssd_conv_w = nrm((L, SSD_CONV, SSD_CONV_DIM), SSD_CONV ** -0.5)
    ssd_conv_b = nrm((L, SSD_CONV_DIM), 0.01)
    dt0 = jnp.exp(jax.random.uniform(next(ks), (L, SSD_HEADS), jnp.float32, math.log(1e-3), math.log(1e-1)))
    ssd_dt_bias = dt0 + jnp.log(-jnp.expm1(-dt0))
    ssd_a_log = jnp.log(jax.random.uniform(next(ks), (L, SSD_HEADS), jnp.float32, 1.0, 16.0))
    ssd_d = gain((L, SSD_HEADS))
    ssd_norm = gain((L, SSD_INNER))
    rw_mu = jax.random.uniform(next(ks), (L, sum(RW_SIZES)), jnp.float32)
    rw_w0 = nrm((L, RW_DIM), 0.5)
    rw_w2 = nrm((L, RW_DECAY_RANK, RW_DIM), RW_DECAY_RANK ** -0.5)
    rw_a0 = nrm((L, RW_DIM), 0.1)
    rw_a2 = nrm((L, RW_A_RANK, RW_DIM), RW_A_RANK ** -0.5)
    rw_g2 = nrm((L, RW_GATE_RANK, RW_DIM), RW_GATE_RANK ** -0.5)
    rw_k_k = 0.85 + nrm((L, RW_DIM), 0.01)
    rw_k_a = gain((L, RW_DIM))
    rw_r_k = nrm((L, RW_DIM), 0.1)
    rw_norm_w = gain((L, RW_DIM))
    rw_norm_b = nrm((L, RW_DIM), 0.01)
    w_branch = nrm((L, N_BRANCH, MIX_WIDTH, D), MIX_WIDTH ** -0.5)
    gate_b = nrm((L, N_BRANCH, D), 0.1)
    w_out = nrm((L, D, D), D ** -0.5)
    ffn2_norm = gain((L, D))
    ffn2_wg = nrm((L, D, F), D ** -0.5)
    ffn2_wu = nrm((L, D, F), D ** -0.5)
    ffn2_wd = nrm((L, F, D), F ** -0.5)
    final_norm = gain((D,))
    return {
        "x": x, "ffn1_norm": ffn1_norm, "ffn1_wg": ffn1_wg, "ffn1_wu": ffn1_wu, "ffn1_wd": ffn1_wd,
        "mix_norm": mix_norm, "w_in": w_in,
        "da_lambda_q1": da_lambda_q1, "da_lambda_k1": da_lambda_k1, "da_lambda_q2": da_lambda_q2,
        "da_lambda_k2": da_lambda_k2, "da_norm": da_norm,
        "gla_gate_w2": gla_gate_w2, "gla_gate_b": gla_gate_b, "gla_norm": gla_norm,
        "ssd_conv_w": ssd_conv_w, "ssd_conv_b": ssd_conv_b, "ssd_dt_bias": ssd_dt_bias, "ssd_a_log": ssd_a_log,
        "ssd_d": ssd_d, "ssd_norm": ssd_norm,
        "rw_mu": rw_mu, "rw_w0": rw_w0, "rw_w2": rw_w2, "rw_a0": rw_a0, "rw_a2": rw_a2, "rw_g2": rw_g2,
        "rw_k_k": rw_k_k, "rw_k_a": rw_k_a, "rw_r_k": rw_r_k, "rw_norm_w": rw_norm_w, "rw_norm_b": rw_norm_b,
        "w_branch": w_branch, "gate_b": gate_b, "w_out": w_out,
        "ffn2_norm": ffn2_norm, "ffn2_wg": ffn2_wg, "ffn2_wu": ffn2_wu, "ffn2_wd": ffn2_wd,
        "final_norm": final_norm,
    }


def reference(x, ffn1_norm, ffn1_wg, ffn1_wu, ffn1_wd, mix_norm, w_in,
              da_lambda_q1, da_lambda_k1, da_lambda_q2, da_lambda_k2, da_norm,
              gla_gate_w2, gla_gate_b, gla_norm,
              ssd_conv_w, ssd_conv_b, ssd_dt_bias, ssd_a_log, ssd_d, ssd_norm,
              rw_mu, rw_w0, rw_w2, rw_a0, rw_a2, rw_g2, rw_k_k, rw_k_a, rw_r_k, rw_norm_w, rw_norm_b,
              w_branch, gate_b, w_out, ffn2_norm, ffn2_wg, ffn2_wu, ffn2_wd, final_norm):
    h = x
    for l in range(DEPTH):
        h = h + 0.5 * swiglu_ffn(rmsnorm(h, ffn1_norm[l]), ffn1_wg[l], ffn1_wu[l], ffn1_wd[l])
        h = h + hybrid_token_mixer(
            rmsnorm(h, mix_norm[l]), l, w_in[l],
            da_lambda_q1[l], da_lambda_k1[l], da_lambda_q2[l], da_lambda_k2[l], da_norm[l],
            gla_gate_w2[l], gla_gate_b[l], gla_norm[l],
            ssd_conv_w[l], ssd_conv_b[l], ssd_dt_bias[l], ssd_a_log[l], ssd_d[l], ssd_norm[l],
            rw_mu[l], rw_w0[l], rw_w2[l], rw_a0[l], rw_a2[l], rw_g2[l], rw_k_k[l], rw_k_a[l], rw_r_k[l],
            rw_norm_w[l], rw_norm_b[l], w_branch[l], gate_b[l], w_out[l])
        h = h + 0.5 * swiglu_ffn(rmsnorm(h, ffn2_norm[l]), ffn2_wg[l], ffn2_wu[l], ffn2_wd[l])
    return rmsnorm(h, final_norm)
```

```python
import functools
import math

import jax
import jax.numpy as jnp
from jax import lax
from jax.experimental import pallas as pl
from jax.experimental.pallas import tpu as pltpu

F32 = jnp.float32
BF16 = jnp.bfloat16
MXU_DTYPE = BF16
HI = lax.Precision.HIGHEST

CHUNK = 64
ROPE_THETA = 10000.0
EPS = 1e-6
DA_HEADS, DA_QK, DA_V = 4, 64, 128
GLA_HEADS, GLA_DK, GLA_DV, GLA_RANK, GLA_TAU = 4, 64, 128, 16, 16.0
SSD_HEADS, SSD_P, SSD_GROUPS, SSD_N, SSD_CONV = 8, 64, 2, 128, 4
SSD_INNER = SSD_HEADS * SSD_P
RW_HEADS, RW_HEAD = 8, 64
RW_DIM = RW_HEADS * RW_HEAD
RW_DECAY_SCALE = 0.606531
RW_GN_EPS = 64e-5
LANE = 128
SUBLANE = 8
VMEM_LIMIT = 48 * 1024 * 1024


def _cparams(sem):
    return pltpu.CompilerParams(dimension_semantics=sem, vmem_limit_bytes=VMEM_LIMIT)


def _mm(a, b):
    return jnp.dot(a.astype(MXU_DTYPE), b.astype(MXU_DTYPE), preferred_element_type=F32)


def _mm_nt(a, b):
    return lax.dot_general(a.astype(MXU_DTYPE), b.astype(MXU_DTYPE), (((1,), (1,)), ((), ())),
                           preferred_element_type=F32)


def _mm_tn(a, b):
    return lax.dot_general(a.astype(MXU_DTYPE), b.astype(MXU_DTYPE), (((0,), (0,)), ((), ())),
                           preferred_element_type=F32)


def _hi(a, b):
    return jnp.dot(a, b, preferred_element_type=F32, precision=HI)


def _hi_nt(a, b):
    return lax.dot_general(a, b, (((1,), (1,)), ((), ())), preferred_element_type=F32, precision=HI)


def _hi_tn(a, b):
    return lax.dot_general(a, b, (((0,), (0,)), ((), ())), preferred_element_type=F32, precision=HI)


def _sigmoid(x):
    return 1.0 / (1.0 + jnp.exp(-x))


def _silu(x):
    return x * _sigmoid(x)


def _softplus(x):
    return jnp.maximum(x, 0.0) + jnp.log1p(jnp.exp(-jnp.abs(x)))


def _rmsnorm_rows(x, g):
    return x * lax.rsqrt(jnp.mean(x * x, axis=-1, keepdims=True) + EPS) * g


def _tri(n, strict=False):
    r = lax.broadcasted_iota(jnp.int32, (n, n), 0)
    c = lax.broadcasted_iota(jnp.int32, (n, n), 1)
    return (c < r) if strict else (c <= r)


def _ffn_kernel(h_ref, g_ref, wg_ref, wu_ref, wd_ref, fg_ref, o_ref, xn_s, acc_s, *, nf, final):
    f = pl.program_id(1)

    @pl.when(f == 0)
    def _():
        xn_s[...] = _rmsnorm_rows(h_ref[...], g_ref[...]).astype(xn_s.dtype)
        acc_s[...] = jnp.zeros_like(acc_s)

    xn = xn_s[...]
    gate = jnp.dot(xn, wg_ref[...], preferred_element_type=F32)
    up = jnp.dot(xn, wu_ref[...], preferred_element_type=F32)
    act = (_silu(gate) * up).astype(wd_ref.dtype)
    acc_s[...] += jnp.dot(act, wd_ref[...], preferred_element_type=F32)

    @pl.when(f == nf - 1)
    def _():
        out = h_ref[...] + 0.5 * acc_s[...]
        if final:
            out = _rmsnorm_rows(out, fg_ref[...])
        o_ref[...] = out


def _ffn(h, g, wg, wu, wd, fg, *, final, tm=512, tf=256):
    T, D = h.shape
    F = wg.shape[1]
    tm = min(tm, T)
    nf = F // tf
    return pl.pallas_call(
        functools.partial(_ffn_kernel, nf=nf, final=final),
        grid=(T // tm, nf),
        in_specs=[
            pl.BlockSpec((tm, D), lambda i, f: (i, 0)),
            pl.BlockSpec((1, D), lambda i, f: (0, 0)),
            pl.BlockSpec((D, tf), lambda i, f: (0, f)),
            pl.BlockSpec((D, tf), lambda i, f: (0, f)),
            pl.BlockSpec((tf, D), lambda i, f: (f, 0)),
            pl.BlockSpec((1, D), lambda i, f: (0, 0)),
        ],
        out_specs=pl.BlockSpec((tm, D), lambda i, f: (i, 0)),
        out_shape=jax.ShapeDtypeStruct((T, D), F32),
        scratch_shapes=[pltpu.VMEM((tm, D), MXU_DTYPE), pltpu.VMEM((tm, D), F32)],
        compiler_params=_cparams(("parallel", "arbitrary")),
        name="ffn",
    )(h, g.reshape(1, D), wg, wu, wd, fg.reshape(1, D))


def _proj_kernel(h_ref, g_ref, w_ref, o_ref):
    xn = _rmsnorm_rows(h_ref[...], g_ref[...]).astype(w_ref.dtype)
    o_ref[...] = jnp.dot(xn, w_ref[...], preferred_element_type=F32).astype(o_ref.dtype)


def _proj(h, g, w, out_dtype, *, tm=512):
    T, D = h.shape
    N = w.shape[1]
    tm = min(tm, T)
    return pl.pallas_call(
        _proj_kernel,
        grid=(T // tm,),
        in_specs=[
            pl.BlockSpec((tm, D), lambda i: (i, 0)),
            pl.BlockSpec((1, D), lambda i: (0, 0)),
            pl.BlockSpec((D, N), lambda i: (0, 0)),
        ],
        out_specs=pl.BlockSpec((tm, N), lambda i: (i, 0)),
        out_shape=jax.ShapeDtypeStruct((T, N), out_dtype),
        compiler_params=_cparams(("parallel",)),
        name="proj",
    )(h, g.reshape(1, D), w)


def _proj_da_kernel(h_ref, g_ref, w_ref, cos_ref, sin_ref, o_ref, *, nqk):
    xn = _rmsnorm_rows(h_ref[...], g_ref[...]).astype(w_ref.dtype)
    p = jnp.dot(xn, w_ref[...], preferred_element_type=F32)
    cos = cos_ref[...]
    sin = sin_ref[...]
    tm = p.shape[0]
    lane = lax.broadcasted_iota(jnp.int32, (tm, LANE), 1)
    first_half = (lane % DA_QK) < (DA_QK // 2)
    qscale = DA_QK ** -0.5
    for blk in range(2 * nqk):
        t = p[:, blk * LANE:(blk + 1) * LANE]
        swapped = jnp.where(first_half, pltpu.roll(t, LANE - DA_QK // 2, 1), pltpu.roll(t, DA_QK // 2, 1))
        t = t * cos + swapped * sin
        if blk < nqk:
            t = t * qscale
        o_ref[:, blk * LANE:(blk + 1) * LANE] = t.astype(o_ref.dtype)
    o_ref[:, 2 * nqk * LANE:] = p[:, 2 * nqk * LANE:].astype(o_ref.dtype)


def _proj_da(h, g, w, cos_t, sin_t, S, *, tm=512):
    T, D = h.shape
    N = w.shape[1]
    tm = min(tm, S)
    ns = S // tm
    nqk = DA_HEADS * 2 * DA_QK // LANE
    return pl.pallas_call(
        functools.partial(_proj_da_kernel, nqk=nqk),
        grid=(T // tm,),
        in_specs=[
            pl.BlockSpec((tm, D), lambda i: (i, 0)),
            pl.BlockSpec((1, D), lambda i: (0, 0)),
            pl.BlockSpec((D, N), lambda i: (0, 0)),
            pl.BlockSpec((tm, LANE), lambda i: (i % ns, 0)),
            pl.BlockSpec((tm, LANE), lambda i: (i % ns, 0)),
        ],
        out_specs=pl.BlockSpec((tm, N), lambda i: (i, 0)),
        out_shape=jax.ShapeDtypeStruct((T, N), MXU_DTYPE),
        compiler_params=_cparams(("parallel",)),
        name="proj_da",
    )(h, g.reshape(1, D), w, cos_t, sin_t)


def _da_kernel(lam_ref, q_ref, k_ref, v_ref, g_ref, o_ref, q1_s, q2_s, m_s, l_s, acc_s, *, tq, out_scale):
    qi = pl.program_id(2)
    kj = pl.program_id(3)

    @pl.when(kj == 0)
    def _():
        q = q_ref[...]
        lane = lax.broadcasted_iota(jnp.int32, q.shape, 1)
        zero = jnp.zeros_like(q)
        q1_s[...] = jnp.where(lane < DA_QK, q, zero)
        q2_s[...] = jnp.where(lane >= DA_QK, q, zero)
        m_s[...] = jnp.full_like(m_s, -jnp.inf)
        l_s[...] = jnp.zeros_like(l_s)
        acc_s[...] = jnp.zeros_like(acc_s)

    def step(masked):
        k = k_ref[...]
        v = v_ref[...]
        if masked:
            row = lax.broadcasted_iota(jnp.int32, (tq, tq), 0) // CHUNK
            col = lax.broadcasted_iota(jnp.int32, (tq, tq), 1) // CHUNK
            allowed = col <= row
        for c, q_s in enumerate((q1_s, q2_s)):
            s = lax.dot_general(q_s[...], k, (((1,), (1,)), ((), ())), preferred_element_type=F32)
            if masked:
                s = jnp.where(allowed, s, -jnp.inf)
            m_prev = m_s[c]
            m_new = jnp.maximum(m_prev, jnp.max(s, axis=-1, keepdims=True))
            alpha = jnp.exp(m_prev - m_new)
            p = jnp.exp(s - m_new)
            l_s[c] = alpha * l_s[c] + jnp.sum(p, axis=-1, keepdims=True)
            acc_s[c] = alpha * acc_s[c] + jnp.dot(p.astype(v.dtype), v, preferred_element_type=F32)
            m_s[c] = m_new

    @pl.when(kj < qi)
    def _():
        step(False)

    @pl.when(kj == qi)
    def _():
        step(True)
        lam = lam_ref[0, 0]
        o = acc_s[0] / l_s[0] - lam * (acc_s[1] / l_s[1])
        o = _rmsnorm_rows(o, g_ref[...]) * out_scale
        o_ref[...] = o.astype(o_ref.dtype)


def _diff_attention(lam, pda, da_norm, B, S, lam_init, *, tq=256):
    T = pda.shape[0]
    tq = min(tq, S)
    nq = S // tq
    H = DA_HEADS
    kernel = functools.partial(_da_kernel, tq=tq, out_scale=1.0 - lam_init)
    return pl.pallas_call(
        kernel,
        grid=(B, H, nq, nq),
        in_specs=[
            pl.BlockSpec(memory_space=pltpu.SMEM),
            pl.BlockSpec((tq, LANE), lambda b, h, i, j: (b * nq + i, h)),
            pl.BlockSpec((tq, LANE), lambda b, h, i, j: (b * nq + jnp.minimum(j, i), H + h)),
            pl.BlockSpec((tq, LANE), lambda b, h, i, j: (b * nq + jnp.minimum(j, i), 2 * H + h)),
            pl.BlockSpec((1, LANE), lambda b, h, i, j: (0, 0)),
        ],
        out_specs=pl.BlockSpec((tq, LANE), lambda b, h, i, j: (b * nq + i, h)),
        out_shape=jax.ShapeDtypeStruct((T, H * DA_V), MXU_DTYPE),
        scratch_shapes=[
            pltpu.VMEM((tq, LANE), MXU_DTYPE),
            pltpu.VMEM((tq, LANE), MXU_DTYPE),
            pltpu.VMEM((2, tq, 1), F32),
            pltpu.VMEM((2, tq, 1), F32),
            pltpu.VMEM((2, tq, DA_V), F32),
        ],
        compiler_params=_cparams(("parallel", "parallel", "parallel", "arbitrary")),
        name="diff_attn",
    )(lam, pda, pda, pda, da_norm.reshape(1, DA_V))


def _gla_kernel(p_ref, w2_ref, gb_ref, ng_ref, o_ref, st_s, lg_s, *, tc):
    i = pl.program_id(1)
    HK = GLA_HEADS * GLA_DK
    HV = GLA_HEADS * GLA_DV

    @pl.when(i == 0)
    def _():
        st_s[...] = jnp.zeros_like(st_s)

    glr = p_ref[:, 2 * HK + 2 * HV:]
    z = _hi(glr, w2_ref[...]) + gb_ref[...]
    lg_s[...] = (jnp.minimum(z, 0.0) - jnp.log1p(jnp.exp(-jnp.abs(z)))) * (1.0 / GLA_TAU)

    tri = _tri(CHUNK).astype(F32)
    causal = _tri(CHUNK)
    lane_k = lax.broadcasted_iota(jnp.int32, (CHUNK, HK), 1) // GLA_DK
    bd = (lax.broadcasted_iota(jnp.int32, (HV, HK), 0) // GLA_DV
          == lax.broadcasted_iota(jnp.int32, (HV, HK), 1) // GLA_DK)
    ng = ng_ref[...]

    def body(c, carry):
        rows = pl.ds(pl.multiple_of(c * CHUNK, CHUNK), CHUNK)
        q = p_ref[rows, 0:HK] * (GLA_DK ** -0.5)
        k = p_ref[rows, HK:2 * HK]
        v = p_ref[rows, 2 * HK:2 * HK + HV]
        og = p_ref[rows, 2 * HK + HV:2 * HK + 2 * HV]
        G = _hi(tri, lg_s[rows, :])
        g_ref_row = G[CHUNK // 2:CHUNK // 2 + 1]
        g_last = G[CHUNK - 1:CHUNK]
        qe = q * jnp.exp(G - g_ref_row)
        ke = k * jnp.exp(g_ref_row - G)
        q_in = q * jnp.exp(G)
        k_end = k * jnp.exp(g_last - G)
        st = st_s[...]
        y_inter = _mm_nt(q_in, st)
        outs = []
        for h in range(GLA_HEADS):
            att = _mm_nt(jnp.where(lane_k == h, qe, 0.0), ke)
            att = jnp.where(causal, att, 0.0)
            vh = v[:, h * GLA_DV:(h + 1) * GLA_DV]
            oh = _mm(att, vh) + y_inter[:, h * GLA_DV:(h + 1) * GLA_DV]
            oh = _rmsnorm_rows(oh, ng)
            outs.append(oh)
        o = jnp.concatenate(outs, axis=1) * _silu(og)
        o_ref[rows, :] = o.astype(o_ref.dtype)
        kv = _mm_tn(v, k_end)
        st_s[...] = st * jnp.exp(g_last) + jnp.where(bd, kv, 0.0)
        return carry

    lax.fori_loop(0, tc // CHUNK, body, 0)


def _gla(pg, w2pad, gate_b, gla_norm, B, S, *, tc=512):
    T, N = pg.shape
    tc = min(tc, S)
    ns = S // tc
    HK, HV = GLA_HEADS * GLA_DK, GLA_HEADS * GLA_DV
    return pl.pallas_call(
        functools.partial(_gla_kernel, tc=tc),
        grid=(B, ns),
        in_specs=[
            pl.BlockSpec((tc, N), lambda b, i: (b * ns + i, 0)),
            pl.BlockSpec((LANE, HK), lambda b, i: (0, 0)),
            pl.BlockSpec((1, HK), lambda b, i: (0, 0)),
            pl.BlockSpec((1, GLA_DV), lambda b, i: (0, 0)),
        ],
        out_specs=pl.BlockSpec((tc, HV), lambda b, i: (b * ns + i, 0)),
        out_shape=jax.ShapeDtypeStruct((T, HV), MXU_DTYPE),
        scratch_shapes=[pltpu.VMEM((HV, HK), F32), pltpu.VMEM((tc, HK), F32)],
        compiler_params=_cparams(("parallel", "arbitrary")),
        name="gla",
    )(pg, w2pad, gate_b.reshape(1, HK), gla_norm.reshape(1, GLA_DV))


def _ssd_kernel(p_ref, cw_ref, cb_ref, dtb_ref, a_ref, e64_ref, e128_ref, dx_ref, ng_ref, o_ref,
                st_s, xpad_s, xs_s, bc_s, ax_s, ax2_s, xdt_s, *, tc):
    i = pl.program_id(1)
    NI = SSD_INNER
    GN = SSD_GROUPS * SSD_N
    CD = NI + 2 * GN
    PAD = SUBLANE

    @pl.when(i == 0)
    def _():
        st_s[...] = jnp.zeros_like(st_s)
        xpad_s[0:PAD, :] = jnp.zeros((PAD, CD), F32)

    xpad_s[PAD:PAD + tc, :] = p_ref[:, NI:NI + CD]
    conv = cb_ref[...]
    for kk in range(SSD_CONV):
        off = PAD - (SSD_CONV - 1) + kk
        conv = conv + cw_ref[kk:kk + 1, :] * xpad_s[off:off + tc, :]
    xpad_s[0:PAD, :] = xpad_s[tc:tc + PAD, :]
    xbc = _silu(conv)
    xs_s[...] = xbc[:, 0:NI]
    bc_s[...] = xbc[:, NI:]
    dt = _softplus(p_ref[:, NI + CD:] + dtb_ref[...])
    dta = dt * a_ref[...]
    ax_s[...] = _hi(dta, e64_ref[...])
    ax2_s[...] = _hi(dta, e128_ref[...])
    xdt_s[...] = xbc[:, 0:NI] * _hi(dt, e64_ref[...])

    tri = _tri(CHUNK).astype(F32)
    tri_t = (lax.broadcasted_iota(jnp.int32, (CHUNK, CHUNK), 0)
             <= lax.broadcasted_iota(jnp.int32, (CHUNK, CHUNK), 1)).astype(F32)
    causal = _tri(CHUNK)
    low_half = lax.broadcasted_iota(jnp.int32, (CHUNK, LANE), 1) < SSD_P
    rep = SSD_HEADS // SSD_GROUPS
    GW = rep * SSD_P

    def body(c, carry):
        rows = pl.ds(pl.multiple_of(c * CHUNK, CHUNK), CHUNK)
        xdt = xdt_s[rows, :]
        acx = _hi(tri, ax_s[rows, :])
        acx2 = _hi(tri, ax2_s[rows, :])
        dta2 = ax2_s[rows, :]
        a_last = acx[CHUNK - 1:CHUNK]
        x_end = xdt * jnp.exp(a_last - acx)
        st = st_s[...]
        y_parts = []
        inter = []
        new_st = []
        for g in range(SSD_GROUPS):
            bm = bc_s[rows, g * SSD_N:(g + 1) * SSD_N]
            cm = bc_s[rows, GN + g * SSD_N:GN + (g + 1) * SSD_N]
            cb = _mm_nt(cm, bm)
            inter.append(_mm(cm, st[:, g * GW:(g + 1) * GW]))
            for pair in range(rep // 2):
                res = []
                for hh in range(2):
                    h = g * rep + pair * 2 + hh
                    col = acx2[:, h * LANE:h * LANE + CHUNK]
                    row = jnp.sum(tri_t * dta2[:, h * LANE:h * LANE + CHUNK], axis=0, keepdims=True)
                    lmat = jnp.exp(jnp.where(causal, col - row, -jnp.inf))
                    blk = (h // 2) * LANE
                    res.append(_mm(cb * lmat, xdt[:, blk:blk + LANE]))
                y_parts.append(jnp.where(low_half, res[0], res[1]))
            new_st.append(_mm_tn(bm, x_end[:, g * GW:(g + 1) * GW]))
        y_intra = jnp.concatenate(y_parts, axis=1)
        y_inter_all = jnp.concatenate(inter, axis=1) * jnp.exp(acx)
        st_s[...] = st * jnp.exp(a_last) + jnp.concatenate(new_st, axis=1)
        y = y_intra + y_inter_all + dx_ref[...] * xs_s[rows, :]
        y = y * _silu(p_ref[rows, 0:NI])
        half = NI // SSD_GROUPS
        outs = [_rmsnorm_rows(y[:, g * half:(g + 1) * half], ng_ref[:, g * half:(g + 1) * half])
                for g in range(SSD_GROUPS)]
        o_ref[rows, :] = jnp.concatenate(outs, axis=1).astype(o_ref.dtype)
        return carry

    lax.fori_loop(0, tc // CHUNK, body, 0)


def _ssd(ps, conv_w, conv_b, dtb_pad, a_pad, e64, e128, d_x, ssd_norm, B, S, *, tc=512):
    T, N = ps.shape
    tc = min(tc, S)
    ns = S // tc
    NI = SSD_INNER
    CD = NI + 2 * SSD_GROUPS * SSD_N
    const = lambda b, i: (0, 0)
    return pl.pallas_call(
        functools.partial(_ssd_kernel, tc=tc),
        grid=(B, ns),
        in_specs=[
            pl.BlockSpec((tc, N), lambda b, i: (b * ns + i, 0)),
            pl.BlockSpec((SSD_CONV, CD), const),
            pl.BlockSpec((1, CD), const),
            pl.BlockSpec((1, LANE), const),
            pl.BlockSpec((1, LANE), const),
            pl.BlockSpec((LANE, NI), const),
            pl.BlockSpec((LANE, SSD_HEADS * LANE), const),
            pl.BlockSpec((1, NI), const),
            pl.BlockSpec((1, NI), const),
        ],
        out_specs=pl.BlockSpec((tc, NI), lambda b, i: (b * ns + i, 0)),
        out_shape=jax.ShapeDtypeStruct((T, NI), MXU_DTYPE),
        scratch_shapes=[
            pltpu.VMEM((SSD_N, NI), F32),
            pltpu.VMEM((tc + 2 * SUBLANE, CD), F32),
            pltpu.VMEM((tc, NI), F32),
            pltpu.VMEM((tc, 2 * SSD_GROUPS * SSD_N), F32),
            pltpu.VMEM((tc, NI), F32),
            pltpu.VMEM((tc, SSD_HEADS * LANE), F32),
            pltpu.VMEM((tc, NI), F32),
        ],
        compiler_params=_cparams(("parallel", "arbitrary")),
        name="ssd",
    )(ps, conv_w, conv_b.reshape(1, CD), dtb_pad, a_pad, e64, e128, d_x, ssd_norm.reshape(1, NI))


def _unit_lower_inverse(a, eye, same_blk):
    d = jnp.where(same_blk, a, 0.0)
    o = a - d
    t = eye + d
    dp = d
    for _ in range(3):
        dp = _hi(dp, dp)
        t = t + _hi(t, dp)
    n = _hi(t, o)
    n2 = _hi(n, n)
    q = eye + n
    q = q + _hi(q, n2)
    return _hi(q, t)


def _rw_kernel(p_ref, mu_ref, w0_ref, w2_ref, a0_ref, a2_ref, g2_ref, kk_ref, ka_ref, rk_ref,
               nw_ref, nb_ref, bd_ref, o_ref,
               mt_s, ppad_s, r_s, k_s, v_s, lw_s, al_s, be_s, g_s, *, tc):
    i = pl.program_id(1)
    R = RW_DIM
    PAD = SUBLANE
    NP = p_ref.shape[1]

    @pl.when(i == 0)
    def _():
        mt_s[...] = jnp.zeros_like(mt_s)
        ppad_s[0:PAD, :] = jnp.zeros((PAD, NP), F32)

    p = p_ref[...]
    ppad_s[PAD:PAD + tc, :] = p
    prev = ppad_s[PAD - 1:PAD - 1 + tc, :]
    ppad_s[0:PAD, :] = ppad_s[tc:tc + PAD, :]
    pm = p + (prev - p) * mu_ref[...]
    r = pm[:, 0:R]
    k = pm[:, R:2 * R]
    v = pm[:, 2 * R:3 * R]
    wa = pm[:, 3 * R:3 * R + LANE]
    gl = pm[:, 3 * R + LANE:3 * R + 2 * LANE]
    lw_s[...] = -RW_DECAY_SCALE * _sigmoid(w0_ref[...] + _hi(jnp.tanh(wa), w2_ref[...]))
    a = _sigmoid(a0_ref[...] + _hi(wa, a2_ref[...]))
    g_s[...] = _mm(_sigmoid(gl), g2_ref[...])
    bd = bd_ref[...]
    kk = k * kk_ref[...]
    kk = kk / jnp.maximum(jnp.sqrt(_hi(kk * kk, bd)), 1e-12)
    r_s[...] = r
    k_s[...] = k * (1.0 + (a - 1.0) * ka_ref[...])
    v_s[...] = v
    al_s[...] = -kk
    be_s[...] = kk * a

    C2 = 2 * CHUNK
    tri = _tri(CHUNK).astype(F32)
    rr = lax.broadcasted_iota(jnp.int32, (C2, C2), 0)
    cc = lax.broadcasted_iota(jnp.int32, (C2, C2), 1)
    strict = cc < rr
    incl = cc <= rr
    eye = (cc == rr).astype(F32)
    same_blk = (rr // 16) == (cc // 16)
    lane = lax.broadcasted_iota(jnp.int32, (CHUNK, LANE), 1)
    m0 = lane < RW_HEAD

    def stack(x):
        return jnp.concatenate([jnp.where(m0, x, 0.0), jnp.where(m0, 0.0, x)], axis=0)

    def body(c, carry):
        rows = pl.ds(pl.multiple_of(c * CHUNK, CHUNK), CHUNK)
        lw = lw_s[rows, :]
        cum = _hi(tri, lw)
        e_cum = jnp.exp(cum)
        e_inv = jnp.exp(-cum)
        ah = al_s[rows, :] * jnp.exp(cum - lw)
        bh = be_s[rows, :] * e_inv
        kh = k_s[rows, :] * e_inv
        rh = r_s[rows, :] * e_cum
        vv = v_s[rows, :]
        w_end = e_cum[CHUNK - 1:CHUNK]
        ys = []
        for j in range(RW_HEADS // 2):
            sl = slice(j * LANE, (j + 1) * LANE)
            a_st, r_st, b_st, k_st, v_st = (stack(x[:, sl]) for x in (ah, rh, bh, kh, vv))
            big = _hi_nt(jnp.concatenate([a_st, r_st], axis=0), jnp.concatenate([b_st, k_st], axis=0))
            a_ab = jnp.where(strict, big[0:C2, 0:C2], 0.0)
            a_ak = jnp.where(strict, big[0:C2, C2:], 0.0)
            r_b = jnp.where(incl, big[C2:, 0:C2], 0.0)
            r_k = jnp.where(incl, big[C2:, C2:], 0.0)
            t_inv = _unit_lower_inverse(a_ab, eye, same_blk)
            mt = mt_s[j]
            rhs = _hi_nt(a_st, mt) + _hi(a_ak, v_st)
            u_st = _hi(t_inv, rhs)
            y_st = _hi_nt(r_st, mt) + _hi(r_b, u_st) + _hi(r_k, v_st)
            ys.append(y_st[0:CHUNK] + y_st[CHUNK:])
            upd = _hi_tn(jnp.concatenate([u_st, v_st], axis=0), jnp.concatenate([b_st, k_st], axis=0))
            mt_s[j] = (mt + upd) * w_end[:, sl]
        y = jnp.concatenate(ys, axis=1)
        mean = _hi(y, bd) * (1.0 / RW_HEAD)
        yc = y - mean
        var = _hi(yc * yc, bd) * (1.0 / RW_HEAD)
        yn = yc * lax.rsqrt(var + RW_GN_EPS) * nw_ref[...] + nb_ref[...]
        bonus = _hi(r_s[rows, :] * k_s[rows, :] * rk_ref[...], bd) * vv
        o_ref[rows, :] = ((yn + bonus) * g_s[rows, :]).astype(o_ref.dtype)
        return carry

    lax.fori_loop(0, tc // CHUNK, body, 0)


def _rwkv(pr, mu, w0, w2pad, a0, a2pad, g2, k_k, k_a, r_k, norm_w, norm_b, bd, B, S, *, tc=256):
    T, N = pr.shape
    tc = min(tc, S)
    ns = S // tc
    R = RW_DIM
    const = lambda b, i: (0, 0)
    row = lambda x: x.reshape(1, -1)
    vec = pl.BlockSpec((1, R), const)
    return pl.pallas_call(
        functools.partial(_rw_kernel, tc=tc),
        grid=(B, ns),
        in_specs=[
            pl.BlockSpec((tc, N), lambda b, i: (b * ns + i, 0)),
            pl.BlockSpec((1, N), const),
            vec,
            pl.BlockSpec((LANE, R), const),
            vec,
            pl.BlockSpec((LANE, R), const),
            pl.BlockSpec((LANE, R), const),
            vec, vec, vec, vec, vec,
            pl.BlockSpec((R, R), const),
        ],
        out_specs=pl.BlockSpec((tc, R), lambda b, i: (b * ns + i, 0)),
        out_shape=jax.ShapeDtypeStruct((T, R), MXU_DTYPE),
        scratch_shapes=[
            pltpu.VMEM((RW_HEADS // 2, LANE, LANE), F32),
            pltpu.VMEM((tc + 2 * SUBLANE, N), F32),
        ] + [pltpu.VMEM((tc, R), F32) for _ in range(7)],
        compiler_params=_cparams(("parallel", "arbitrary")),
        name="rwkv7",
    )(pr, row(mu), row(w0), w2pad, row(a0), a2pad, g2, row(k_k), row(k_a), row(r_k),
      row(norm_w), row(norm_b), bd)


def _merge_kernel(h_ref, g_ref, wgate_ref, gb_ref, ya_ref, yb_ref, yc_ref, yd_ref, wb_ref, wo_ref, o_ref):
    h = h_ref[...]
    D = h.shape[1]
    xn = _rmsnorm_rows(h, g_ref[...]).astype(wgate_ref.dtype)
    merged = jnp.zeros(h.shape, F32)
    for n, y_ref in enumerate((ya_ref, yb_ref, yc_ref, yd_ref)):
        logits = jnp.dot(xn, wgate_ref[:, n * D:(n + 1) * D], preferred_element_type=F32) + gb_ref[n:n + 1, :]
        proj = jnp.dot(y_ref[...], wb_ref[n], preferred_element_type=F32)
        merged = merged + _sigmoid(logits) * proj
    o_ref[...] = h + jnp.dot(merged.astype(wo_ref.dtype), wo_ref[...], preferred_element_type=F32)


def _merge(h, g, wgate, gate_b, ys, w_branch, w_out, *, tm=256):
    T, D = h.shape
    tm = min(tm, T)
    NB, MW, _ = w_branch.shape
    const2 = lambda i: (0, 0)
    ytile = pl.BlockSpec((tm, MW), lambda i: (i, 0))
    return pl.pallas_call(
        _merge_kernel,
        grid=(T // tm,),
        in_specs=[
            pl.BlockSpec((tm, D), lambda i: (i, 0)),
            pl.BlockSpec((1, D), const2),
            pl.BlockSpec((D, NB * D), const2),
            pl.BlockSpec((NB, D), const2),
            ytile, ytile, ytile, ytile,
            pl.BlockSpec((NB, MW, D), lambda i: (0, 0, 0)),
            pl.BlockSpec((D, D), const2),
        ],
        out_specs=pl.BlockSpec((tm, D), lambda i: (i, 0)),
        out_shape=jax.ShapeDtypeStruct((T, D), F32),
        compiler_params=_cparams(("parallel",)),
        name="merge",
    )(h, g.reshape(1, D), wgate, gate_b, *ys, w_branch, w_out)


def _pad_cols(w, n):
    return jnp.pad(w, ((0, 0), (0, n - w.shape[1])))


def _pad_rows_at(w, start, total):
    return jnp.pad(w, ((start, total - start - w.shape[0]), (0, 0)))


def _rope_tables(S):
    half = DA_QK // 2
    inv_freq = ROPE_THETA ** (-jnp.arange(half, dtype=F32) / half)
    ang = jnp.arange(S, dtype=F32)[:, None] * inv_freq[None, :]
    cos, sin = jnp.cos(ang), jnp.sin(ang)
    reps = LANE // DA_QK
    return jnp.tile(jnp.concatenate([cos, cos], axis=1), (1, reps)), jnp.tile(jnp.concatenate([-sin, sin], axis=1), (1, reps))


def _head_expand(heads, width):
    r = jnp.arange(LANE)[:, None]
    c = jnp.arange(heads * width)[None, :] // width
    return (r == c).astype(F32)


def kernel(x, ffn1_norm, ffn1_wg, ffn1_wu, ffn1_wd, mix_norm, w_in, da_lambda_q1, da_lambda_k1, da_lambda_q2, da_lambda_k2, da_norm, gla_gate_w2, gla_gate_b, gla_norm, ssd_conv_w, ssd_conv_b, ssd_dt_bias, ssd_a_log, ssd_d, ssd_norm, rw_mu, rw_w0, rw_w2, rw_a0, rw_a2, rw_g2, rw_k_k, rw_k_a, rw_r_k, rw_norm_w, rw_norm_b, w_branch, gate_b, w_out, ffn2_norm, ffn2_wg, ffn2_wu, ffn2_wd, final_norm):
    B, S, D = x.shape
    depth = w_in.shape[0]
    T = B * S
    h = x.reshape(T, D)
    cos_t, sin_t = _rope_tables(S)
    e64 = _head_expand(SSD_HEADS, SSD_P)
    e128 = _head_expand(SSD_HEADS, LANE)
    rw_bd = (jnp.arange(RW_DIM)[:, None] // RW_HEAD == jnp.arange(RW_DIM)[None, :] // RW_HEAD).astype(F32)

    n_da = DA_HEADS * (4 * DA_QK + DA_V)
    n_gla = 2 * GLA_HEADS * GLA_DK + 2 * GLA_HEADS * GLA_DV + GLA_RANK
    n_ssd = SSD_INNER + (SSD_INNER + 2 * SSD_GROUPS * SSD_N) + SSD_HEADS
    n_rw = 3 * RW_DIM + rw_w2.shape[1] + rw_a2.shape[1] + rw_g2.shape[1]
    o_gla, o_ssd, o_rw, o_gate = n_da, n_da + n_gla, n_da + n_gla + n_ssd, n_da + n_gla + n_ssd + n_rw
    pad128 = lambda n: -(-n // LANE) * LANE
    cast = lambda w: w.astype(MXU_DTYPE)

    for l in range(depth):
        h = _ffn(h, ffn1_norm[l], cast(ffn1_wg[l]), cast(ffn1_wu[l]), cast(ffn1_wd[l]), final_norm, final=False)

        wl = w_in[l]
        w_da = cast(wl[:, 0:n_da])
        w_gla = cast(_pad_cols(wl[:, o_gla:o_ssd], pad128(n_gla)))
        w_ssd = cast(_pad_cols(wl[:, o_ssd:o_rw], pad128(n_ssd)))
        w_rw = cast(wl[:, o_rw:o_gate])
        w_gate = cast(wl[:, o_gate:])
        g = mix_norm[l]

        lam_init = 0.8 - 0.6 * math.exp(-0.3 * l)
        lam = (jnp.exp(jnp.sum(da_lambda_q1[l] * da_lambda_k1[l])) - jnp.exp(jnp.sum(da_lambda_q2[l] * da_lambda_k2[l]))
               + lam_init).reshape(1, 1).astype(F32)
        pda = _proj_da(h, g, w_da, cos_t, sin_t, S)
        y_a = _diff_attention(lam, pda, da_norm[l], B, S, lam_init)

        pg = _proj(h, g, w_gla, F32)
        y_b = _gla(pg, _pad_rows_at(gla_gate_w2[l], 0, LANE), gla_gate_b[l], gla_norm[l], B, S)

        ps = _proj(h, g, w_ssd, F32)
        dtb_pad = _pad_cols(ssd_dt_bias[l].reshape(1, -1), LANE)
        a_pad = _pad_cols(-jnp.exp(ssd_a_log[l]).reshape(1, -1), LANE)
        d_x = jnp.repeat(ssd_d[l], SSD_P).reshape(1, -1)
        y_c = _ssd(ps, ssd_conv_w[l], ssd_conv_b[l], dtb_pad, a_pad, e64, e128, d_x, ssd_norm[l], B, S)

        pr = _proj(h, g, w_rw, F32)
        rank_w = rw_w2.shape[1]
        y_d = _rwkv(pr, rw_mu[l], rw_w0[l], _pad_rows_at(rw_w2[l], 0, LANE), rw_a0[l],
                    _pad_rows_at(rw_a2[l], rank_w, LANE), cast(rw_g2[l]), rw_k_k[l], rw_k_a[l], rw_r_k[l],
                    rw_norm_w[l], rw_norm_b[l], rw_bd, B, S)

        h = _merge(h, g, w_gate, gate_b[l], (y_a, y_b, y_c, y_d), cast(w_branch[l]), cast(w_out[l]))
        h = _ffn(h, ffn2_norm[l], cast(ffn2_wg[l]), cast(ffn2_wu[l]), cast(ffn2_wd[l]), final_norm,
                 final=(l == depth - 1))
    return h.reshape(B, S, D)
```

```python
import functools
import math

import jax
import jax.numpy as jnp
from jax import lax
from jax.experimental import pallas as pl
from jax.experimental.pallas import tpu as pltpu

F32 = jnp.float32
BF16 = jnp.bfloat16
MXU_DTYPE = BF16

CHUNK = 64
ROPE_THETA = 10000.0
EPS = 1e-6
DA_HEADS, DA_QK, DA_V = 4, 64, 128
GLA_HEADS, GLA_DK, GLA_DV, GLA_RANK, GLA_TAU = 4, 64, 128, 16, 16.0
SSD_HEADS, SSD_P, SSD_GROUPS, SSD_N, SSD_CONV = 8, 64, 2, 128, 4
SSD_INNER = SSD_HEADS * SSD_P
RW_HEADS, RW_HEAD = 8, 64
RW_DIM = RW_HEADS * RW_HEAD
RW_DECAY_SCALE = 0.606531
RW_GN_EPS = 64e-5
LANE = 128
SUBLANE = 8
VMEM_LIMIT = 48 * 1024 * 1024


def _cparams(sem):
    return pltpu.CompilerParams(dimension_semantics=sem, vmem_limit_bytes=VMEM_LIMIT)


def _mm(a, b):
    return jnp.dot(a.astype(MXU_DTYPE), b.astype(MXU_DTYPE), preferred_element_type=F32)


def _mm_nt(a, b):
    return lax.dot_general(a.astype(MXU_DTYPE), b.astype(MXU_DTYPE), (((1,), (1,)), ((), ())),
                           preferred_element_type=F32)


def _mm_tn(a, b):
    return lax.dot_general(a.astype(MXU_DTYPE), b.astype(MXU_DTYPE), (((0,), (0,)), ((), ())),
                           preferred_element_type=F32)


def _split(x):
    hi = x.astype(MXU_DTYPE)
    lo = (x - hi.astype(F32)).astype(MXU_DTYPE)
    return hi, lo


def _sel_l(m01, x):
    hi, lo = _split(x)
    m = m01.astype(MXU_DTYPE)
    return jnp.dot(m, hi, preferred_element_type=F32) + jnp.dot(m, lo, preferred_element_type=F32)


def _sel_r(x, m01):
    hi, lo = _split(x)
    m = m01.astype(MXU_DTYPE)
    return jnp.dot(hi, m, preferred_element_type=F32) + jnp.dot(lo, m, preferred_element_type=F32)


def _sigmoid(x):
    return 1.0 / (1.0 + jnp.exp(-x))


def _silu(x):
    return x * _sigmoid(x)


def _softplus(x):
    return jnp.maximum(x, 0.0) + jnp.log1p(jnp.exp(-jnp.abs(x)))


def _rmsnorm_rows(x, g):
    return x * lax.rsqrt(jnp.mean(x * x, axis=-1, keepdims=True) + EPS) * g


def _tri(n, strict=False):
    r = lax.broadcasted_iota(jnp.int32, (n, n), 0)
    c = lax.broadcasted_iota(jnp.int32, (n, n), 1)
    return (c < r) if strict else (c <= r)


def _ffn_kernel(h_ref, g_ref, wg_ref, wu_ref, wd_ref, fg_ref, o_ref, xn_s, acc_s, *, nf, final):
    f = pl.program_id(1)

    @pl.when(f == 0)
    def _():
        xn_s[...] = _rmsnorm_rows(h_ref[...], g_ref[...]).astype(xn_s.dtype)
        acc_s[...] = jnp.zeros_like(acc_s)

    xn = xn_s[...]
    gate = jnp.dot(xn, wg_ref[...], preferred_element_type=F32)
    up = jnp.dot(xn, wu_ref[...], preferred_element_type=F32)
    act = (_silu(gate) * up).astype(wd_ref.dtype)
    acc_s[...] += jnp.dot(act, wd_ref[...], preferred_element_type=F32)

    @pl.when(f == nf - 1)
    def _():
        out = h_ref[...] + 0.5 * acc_s[...]
        if final:
            out = _rmsnorm_rows(out, fg_ref[...])
        o_ref[...] = out


def _ffn(h, g, wg, wu, wd, fg, *, final, tm=512, tf=256):
    T, D = h.shape
    F = wg.shape[1]
    tm = min(tm, T)
    nf = F // tf
    return pl.pallas_call(
        functools.partial(_ffn_kernel, nf=nf, final=final),
        grid=(T // tm, nf),
        in_specs=[
            pl.BlockSpec((tm, D), lambda i, f: (i, 0)),
            pl.BlockSpec((1, D), lambda i, f: (0, 0)),
            pl.BlockSpec((D, tf), lambda i, f: (0, f)),
            pl.BlockSpec((D, tf), lambda i, f: (0, f)),
            pl.BlockSpec((tf, D), lambda i, f: (f, 0)),
            pl.BlockSpec((1, D), lambda i, f: (0, 0)),
        ],
        out_specs=pl.BlockSpec((tm, D), lambda i, f: (i, 0)),
        out_shape=jax.ShapeDtypeStruct((T, D), F32),
        scratch_shapes=[pltpu.VMEM((tm, D), MXU_DTYPE), pltpu.VMEM((tm, D), F32)],
        compiler_params=_cparams(("parallel", "arbitrary")),
        name="ffn",
    )(h, g.reshape(1, D), wg, wu, wd, fg.reshape(1, D))


def _proj_kernel(h_ref, g_ref, w_ref, o_ref):
    xn = _rmsnorm_rows(h_ref[...], g_ref[...]).astype(w_ref.dtype)
    o_ref[...] = jnp.dot(xn, w_ref[...], preferred_element_type=F32).astype(o_ref.dtype)


def _proj(h, g, w, out_dtype, *, tm=512):
    T, D = h.shape
    N = w.shape[1]
    tm = min(tm, T)
    return pl.pallas_call(
        _proj_kernel,
        grid=(T // tm,),
        in_specs=[
            pl.BlockSpec((tm, D), lambda i: (i, 0)),
            pl.BlockSpec((1, D), lambda i: (0, 0)),
            pl.BlockSpec((D, N), lambda i: (0, 0)),
        ],
        out_specs=pl.BlockSpec((tm, N), lambda i: (i, 0)),
        out_shape=jax.ShapeDtypeStruct((T, N), out_dtype),
        compiler_params=_cparams(("parallel",)),
        name="proj",
    )(h, g.reshape(1, D), w)


def _proj_da_kernel(h_ref, g_ref, w_ref, cos_ref, sin_ref, o_ref, *, nqk):
    xn = _rmsnorm_rows(h_ref[...], g_ref[...]).astype(w_ref.dtype)
    p = jnp.dot(xn, w_ref[...], preferred_element_type=F32)
    cos = cos_ref[...]
    sin = sin_ref[...]
    tm = p.shape[0]
    lane = lax.broadcasted_iota(jnp.int32, (tm, LANE), 1)
    first_half = (lane % DA_QK) < (DA_QK // 2)
    qscale = DA_QK ** -0.5
    for blk in range(2 * nqk):
        t = p[:, blk * LANE:(blk + 1) * LANE]
        swapped = jnp.where(first_half, pltpu.roll(t, LANE - DA_QK // 2, 1), pltpu.roll(t, DA_QK // 2, 1))
        t = t * cos + swapped * sin
        if blk < nqk:
            t = t * qscale
        o_ref[:, blk * LANE:(blk + 1) * LANE] = t.astype(o_ref.dtype)
    o_ref[:, 2 * nqk * LANE:] = p[:, 2 * nqk * LANE:].astype(o_ref.dtype)


def _proj_da(h, g, w, cos_t, sin_t, S, *, tm=512):
    T, D = h.shape
    N = w.shape[1]
    tm = min(tm, S)
    ns = S // tm
    nqk = DA_HEADS * 2 * DA_QK // LANE
    return pl.pallas_call(
        functools.partial(_proj_da_kernel, nqk=nqk),
        grid=(T // tm,),
        in_specs=[
            pl.BlockSpec((tm, D), lambda i: (i, 0)),
            pl.BlockSpec((1, D), lambda i: (0, 0)),
            pl.BlockSpec((D, N), lambda i: (0, 0)),
            pl.BlockSpec((tm, LANE), lambda i: (i % ns, 0)),
            pl.BlockSpec((tm, LANE), lambda i: (i % ns, 0)),
        ],
        out_specs=pl.BlockSpec((tm, N), lambda i: (i, 0)),
        out_shape=jax.ShapeDtypeStruct((T, N), MXU_DTYPE),
        compiler_params=_cparams(("parallel",)),
        name="proj_da",
    )(h, g.reshape(1, D), w, cos_t, sin_t)


def _da_kernel(lam_ref, q_ref, k_ref, v_ref, g_ref, o_ref, q_s, m_s, l_s, acc_s, *, tq, nsub, out_scale):
    qi = pl.program_id(2)
    q = q_ref[...]
    lane = lax.broadcasted_iota(jnp.int32, q.shape, 1)
    zero = jnp.zeros_like(q)
    q_s[0] = jnp.where(lane < DA_QK, q, zero)
    q_s[1] = jnp.where(lane >= DA_QK, q, zero)
    m_s[...] = jnp.full_like(m_s, -jnp.inf)
    l_s[...] = jnp.zeros_like(l_s)
    acc_s[...] = jnp.zeros_like(acc_s)
    tr = tq // nsub
    reps = tq // LANE
    chains = [(c, pl.ds(r * tr, tr), r) for r in range(nsub) for c in range(2)]

    def step(j, masked):
        rows = pl.ds(pl.multiple_of(j * tq, tq), tq)
        k = k_ref[rows, :]
        v = v_ref[rows, :]
        s = [lax.dot_general(q_s[c, rs, :], k, (((1,), (1,)), ((), ())), preferred_element_type=F32)
             for c, rs, _ in chains]
        if masked:
            col = lax.broadcasted_iota(jnp.int32, (tr, tq), 1) // CHUNK
            row = lax.broadcasted_iota(jnp.int32, (tr, tq), 0)
            s = [jnp.where(col <= (row + r * tr) // CHUNK, x, -jnp.inf) for x, (_, _, r) in zip(s, chains)]
        m_prev = [m_s[c, rs, :] for c, rs, _ in chains]
        m_new = [jnp.maximum(mp, jnp.max(x, axis=-1, keepdims=True)) for mp, x in zip(m_prev, s)]
        p = [jnp.exp(x - jnp.concatenate([mn] * reps, axis=1)) for x, mn in zip(s, m_new)]
        alpha = [jnp.exp(mp - mn) for mp, mn in zip(m_prev, m_new)]
        pv = [jnp.dot(x.astype(v.dtype), v, preferred_element_type=F32) for x in p]
        for (c, rs, _), mn, a, x, o in zip(chains, m_new, alpha, p, pv):
            m_s[c, rs, :] = mn
            l_s[c, rs, :] = a * l_s[c, rs, :] + jnp.sum(x, axis=-1, keepdims=True)
            acc_s[c, rs, :] = a * acc_s[c, rs, :] + o

    def full_block(j, carry):
        step(j, False)
        return carry

    lax.fori_loop(0, qi, full_block, 0)
    step(qi, True)
    lam = lam_ref[0, 0]
    o = acc_s[0] / l_s[0] - lam * (acc_s[1] / l_s[1])
    o = _rmsnorm_rows(o, g_ref[...]) * out_scale
    o_ref[...] = o.astype(o_ref.dtype)


def _diff_attention(lam, pda, da_norm, B, S, lam_init, *, tq=512, nsub=2):
    T = pda.shape[0]
    tq = min(tq, S)
    nq = S // tq
    H = DA_HEADS
    kernel = functools.partial(_da_kernel, tq=tq, nsub=nsub, out_scale=1.0 - lam_init)
    return pl.pallas_call(
        kernel,
        grid=(B, H, nq),
        in_specs=[
            pl.BlockSpec(memory_space=pltpu.SMEM),
            pl.BlockSpec((tq, LANE), lambda b, h, i: (b * nq + i, h)),
            pl.BlockSpec((S, LANE), lambda b, h, i: (b, H + h)),
            pl.BlockSpec((S, LANE), lambda b, h, i: (b, 2 * H + h)),
            pl.BlockSpec((1, LANE), lambda b, h, i: (0, 0)),
        ],
        out_specs=pl.BlockSpec((tq, LANE), lambda b, h, i: (b * nq + i, h)),
        out_shape=jax.ShapeDtypeStruct((T, H * DA_V), MXU_DTYPE),
        scratch_shapes=[
            pltpu.VMEM((2, tq, LANE), MXU_DTYPE),
            pltpu.VMEM((2, tq, LANE), F32),
            pltpu.VMEM((2, tq, LANE), F32),
            pltpu.VMEM((2, tq, DA_V), F32),
        ],
        compiler_params=_cparams(("parallel", "parallel", "arbitrary")),
        name="diff_attn",
    )(lam, pda, pda, pda, da_norm.reshape(1, DA_V))


def _gla_kernel(p_ref, w2_ref, gb_ref, ng_ref, o_ref, st_s, lg_s, *, tc):
    i = pl.program_id(1)
    HK = GLA_HEADS * GLA_DK
    HV = GLA_HEADS * GLA_DV

    @pl.when(i == 0)
    def _():
        st_s[...] = jnp.zeros_like(st_s)

    glr = p_ref[:, 2 * HK + 2 * HV:]
    z = _mm(glr, w2_ref[...]) + gb_ref[...]
    lg_s[...] = (jnp.minimum(z, 0.0) - jnp.log1p(jnp.exp(-jnp.abs(z)))) * (1.0 / GLA_TAU)

    tri = _tri(CHUNK).astype(F32)
    causal = _tri(CHUNK)
    lane_k = lax.broadcasted_iota(jnp.int32, (CHUNK, HK), 1) // GLA_DK
    bd = (lax.broadcasted_iota(jnp.int32, (HV, HK), 0) // GLA_DV
          == lax.broadcasted_iota(jnp.int32, (HV, HK), 1) // GLA_DK)
    ng = ng_ref[...]

    def body(c, carry):
        rows = pl.ds(pl.multiple_of(c * CHUNK, CHUNK), CHUNK)
        q = p_ref[rows, 0:HK] * (GLA_DK ** -0.5)
        k = p_ref[rows, HK:2 * HK]
        v = p_ref[rows, 2 * HK:2 * HK + HV]
        og = p_ref[rows, 2 * HK + HV:2 * HK + 2 * HV]
        G = _sel_l(tri, lg_s[rows, :])
        g_ref_row = G[CHUNK // 2:CHUNK // 2 + 1]
        g_last = G[CHUNK - 1:CHUNK]
        qe = q * jnp.exp(G - g_ref_row)
        ke = k * jnp.exp(g_ref_row - G)
        q_in = q * jnp.exp(G)
        k_end = k * jnp.exp(g_last - G)
        st = st_s[...]
        y_inter = _mm_nt(q_in, st)
        outs = []
        for h in range(GLA_HEADS):
            att = _mm_nt(jnp.where(lane_k == h, qe, 0.0), ke)
            att = jnp.where(causal, att, 0.0)
            vh = v[:, h * GLA_DV:(h + 1) * GLA_DV]
            oh = _mm(att, vh) + y_inter[:, h * GLA_DV:(h + 1) * GLA_DV]
            oh = _rmsnorm_rows(oh, ng)
            outs.append(oh)
        o = jnp.concatenate(outs, axis=1) * _silu(og)
        o_ref[rows, :] = o.astype(o_ref.dtype)
        kv = _mm_tn(v, k_end)
        st_s[...] = st * jnp.exp(g_last) + jnp.where(bd, kv, 0.0)
        return carry

    lax.fori_loop(0, tc // CHUNK, body, 0)


def _gla(pg, w2pad, gate_b, gla_norm, B, S, *, tc=512):
    T, N = pg.shape
    tc = min(tc, S)
    ns = S // tc
    HK, HV = GLA_HEADS * GLA_DK, GLA_HEADS * GLA_DV
    return pl.pallas_call(
        functools.partial(_gla_kernel, tc=tc),
        grid=(B, ns),
        in_specs=[
            pl.BlockSpec((tc, N), lambda b, i: (b * ns + i, 0)),
            pl.BlockSpec((LANE, HK), lambda b, i: (0, 0)),
            pl.BlockSpec((1, HK), lambda b, i: (0, 0)),
            pl.BlockSpec((1, GLA_DV), lambda b, i: (0, 0)),
        ],
        out_specs=pl.BlockSpec((tc, HV), lambda b, i: (b * ns + i, 0)),
        out_shape=jax.ShapeDtypeStruct((T, HV), MXU_DTYPE),
        scratch_shapes=[pltpu.VMEM((HV, HK), F32), pltpu.VMEM((tc, HK), F32)],
        compiler_params=_cparams(("parallel", "arbitrary")),
        name="gla",
    )(pg, w2pad, gate_b.reshape(1, HK), gla_norm.reshape(1, GLA_DV))


def _ssd_kernel(p_ref, cw_ref, cb_ref, dtb_ref, a_ref, e64_ref, e128_ref, dx_ref, ng_ref, o_ref,
                st_s, xpad_s, xs_s, bc_s, ax_s, ax2_s, xdt_s, *, tc):
    i = pl.program_id(1)
    NI = SSD_INNER
    GN = SSD_GROUPS * SSD_N
    CD = NI + 2 * GN
    PAD = SUBLANE

    @pl.when(i == 0)
    def _():
        st_s[...] = jnp.zeros_like(st_s)
        xpad_s[0:PAD, :] = jnp.zeros((PAD, CD), F32)

    xpad_s[PAD:PAD + tc, :] = p_ref[:, NI:NI + CD]
    conv = cb_ref[...]
    for kk in range(SSD_CONV):
        off = PAD - (SSD_CONV - 1) + kk
        conv = conv + cw_ref[kk:kk + 1, :] * xpad_s[off:off + tc, :]
    xpad_s[0:PAD, :] = xpad_s[tc:tc + PAD, :]
    xbc = _silu(conv)
    xs_s[...] = xbc[:, 0:NI]
    bc_s[...] = xbc[:, NI:]
    dt = _softplus(p_ref[:, NI + CD:] + dtb_ref[...])
    dta = dt * a_ref[...]
    ax_s[...] = _sel_r(dta, e64_ref[...])
    ax2_s[...] = _sel_r(dta, e128_ref[...])
    xdt_s[...] = xbc[:, 0:NI] * _sel_r(dt, e64_ref[...])

    tri = _tri(CHUNK).astype(F32)
    tri_t = (lax.broadcasted_iota(jnp.int32, (CHUNK, CHUNK), 0)
             <= lax.broadcasted_iota(jnp.int32, (CHUNK, CHUNK), 1)).astype(F32)
    causal = _tri(CHUNK)
    low_half = lax.broadcasted_iota(jnp.int32, (CHUNK, LANE), 1) < SSD_P
    rep = SSD_HEADS // SSD_GROUPS
    GW = rep * SSD_P

    def body(c, carry):
        rows = pl.ds(pl.multiple_of(c * CHUNK, CHUNK), CHUNK)
        xdt = xdt_s[rows, :]
        acx = _sel_l(tri, ax_s[rows, :])
        acx2 = _sel_l(tri, ax2_s[rows, :])
        dta2 = ax2_s[rows, :]
        a_last = acx[CHUNK - 1:CHUNK]
        x_end = xdt * jnp.exp(a_last - acx)
        st = st_s[...]
        y_parts = []
        inter = []
        new_st = []
        for g in range(SSD_GROUPS):
            bm = bc_s[rows, g * SSD_N:(g + 1) * SSD_N]
            cm = bc_s[rows, GN + g * SSD_N:GN + (g + 1) * SSD_N]
            cb = _mm_nt(cm, bm)
            inter.append(_mm(cm, st[:, g * GW:(g + 1) * GW]))
            for pair in range(rep // 2):
                res = []
                for hh in range(2):
                    h = g * rep + pair * 2 + hh
                    col = acx2[:, h * LANE:h * LANE + CHUNK]
                    row = jnp.sum(tri_t * dta2[:, h * LANE:h * LANE + CHUNK], axis=0, keepdims=True)
                    lmat = jnp.exp(jnp.where(causal, col - row, -jnp.inf))
                    blk = (h // 2) * LANE
                    res.append(_mm(cb * lmat, xdt[:, blk:blk + LANE]))
                y_parts.append(jnp.where(low_half, res[0], res[1]))
            new_st.append(_mm_tn(bm, x_end[:, g * GW:(g + 1) * GW]))
        y_intra = jnp.concatenate(y_parts, axis=1)
        y_inter_all = jnp.concatenate(inter, axis=1) * jnp.exp(acx)
        st_s[...] = st * jnp.exp(a_last) + jnp.concatenate(new_st, axis=1)
        y = y_intra + y_inter_all + dx_ref[...] * xs_s[rows, :]
        y = y * _silu(p_ref[rows, 0:NI])
        half = NI // SSD_GROUPS
        outs = [_rmsnorm_rows(y[:, g * half:(g + 1) * half], ng_ref[:, g * half:(g + 1) * half])
                for g in range(SSD_GROUPS)]
        o_ref[rows, :] = jnp.concatenate(outs, axis=1).astype(o_ref.dtype)
        return carry

    lax.fori_loop(0, tc // CHUNK, body, 0)


def _ssd(ps, conv_w, conv_b, dtb_pad, a_pad, e64, e128, d_x, ssd_norm, B, S, *, tc=512):
    T, N = ps.shape
    tc = min(tc, S)
    ns = S // tc
    NI = SSD_INNER
    CD = NI + 2 * SSD_GROUPS * SSD_N
    const = lambda b, i: (0, 0)
    return pl.pallas_call(
        functools.partial(_ssd_kernel, tc=tc),
        grid=(B, ns),
        in_specs=[
            pl.BlockSpec((tc, N), lambda b, i: (b * ns + i, 0)),
            pl.BlockSpec((SSD_CONV, CD), const),
            pl.BlockSpec((1, CD), const),
            pl.BlockSpec((1, LANE), const),
            pl.BlockSpec((1, LANE), const),
            pl.BlockSpec((LANE, NI), const),
            pl.BlockSpec((LANE, SSD_HEADS * LANE), const),
            pl.BlockSpec((1, NI), const),
            pl.BlockSpec((1, NI), const),
        ],
        out_specs=pl.BlockSpec((tc, NI), lambda b, i: (b * ns + i, 0)),
        out_shape=jax.ShapeDtypeStruct((T, NI), MXU_DTYPE),
        scratch_shapes=[
            pltpu.VMEM((SSD_N, NI), F32),
            pltpu.VMEM((tc + 2 * SUBLANE, CD), F32),
            pltpu.VMEM((tc, NI), F32),
            pltpu.VMEM((tc, 2 * SSD_GROUPS * SSD_N), F32),
            pltpu.VMEM((tc, NI), F32),
            pltpu.VMEM((tc, SSD_HEADS * LANE), F32),
            pltpu.VMEM((tc, NI), F32),
        ],
        compiler_params=_cparams(("parallel", "arbitrary")),
        name="ssd",
    )(ps, conv_w, conv_b.reshape(1, CD), dtb_pad, a_pad, e64, e128, d_x, ssd_norm.reshape(1, NI))


def _unit_lower_inverse(mats, eye, same_blk):
    d = [jnp.where(same_blk, a, 0.0) for a in mats]
    o = [a - x for a, x in zip(mats, d)]
    t = [eye + x for x in d]
    dp = d
    for _ in range(3):
        dp = [_mm(x, x) for x in dp]
        t = [x + _mm(x, y) for x, y in zip(t, dp)]
    n = [_mm(x, y) for x, y in zip(t, o)]
    n2 = [_mm(x, x) for x in n]
    q = [eye + x for x in n]
    q = [x + _mm(x, y) for x, y in zip(q, n2)]
    return [_mm(x, y) for x, y in zip(q, t)]


def _rw_kernel(p_ref, mu_ref, w0_ref, w2_ref, a0_ref, a2_ref, g2_ref, kk_ref, ka_ref, rk_ref,
               nw_ref, nb_ref, bd_ref, o_ref,
               mt_s, ppad_s, r_s, k_s, v_s, lw_s, al_s, be_s, g_s, *, tc, nb):
    i = pl.program_id(0)
    R = RW_DIM
    PAD = SUBLANE
    NP = p_ref.shape[2]
    NPAIR = RW_HEADS // 2

    @pl.when(i == 0)
    def _():
        mt_s[...] = jnp.zeros_like(mt_s)
        ppad_s[:, 0:PAD, :] = jnp.zeros((nb, PAD, NP), F32)

    bd = bd_ref[...]
    for b in range(nb):
        p = p_ref[b]
        ppad_s[b, PAD:PAD + tc, :] = p
        prev = ppad_s[b, PAD - 1:PAD - 1 + tc, :]
        ppad_s[b, 0:PAD, :] = ppad_s[b, tc:tc + PAD, :]
        pm = p + (prev - p) * mu_ref[...]
        r = pm[:, 0:R]
        k = pm[:, R:2 * R]
        v = pm[:, 2 * R:3 * R]
        wa = pm[:, 3 * R:3 * R + LANE]
        gl = pm[:, 3 * R + LANE:3 * R + 2 * LANE]
        lw_s[b] = -RW_DECAY_SCALE * _sigmoid(w0_ref[...] + _mm(jnp.tanh(wa), w2_ref[...]))
        a = _sigmoid(a0_ref[...] + _mm(wa, a2_ref[...]))
        g_s[b] = _mm(_sigmoid(gl), g2_ref[...])
        kk = k * kk_ref[...]
        kk = kk / jnp.maximum(jnp.sqrt(_sel_r(kk * kk, bd)), 1e-12)
        r_s[b] = r
        k_s[b] = k * (1.0 + (a - 1.0) * ka_ref[...])
        v_s[b] = v
        al_s[b] = -kk
        be_s[b] = kk * a

    C2 = 2 * CHUNK
    tri = _tri(CHUNK).astype(F32)
    rr = lax.broadcasted_iota(jnp.int32, (C2, C2), 0)
    cc = lax.broadcasted_iota(jnp.int32, (C2, C2), 1)
    strict = cc < rr
    incl = cc <= rr
    eye = (cc == rr).astype(F32)
    same_blk = (rr // 16) == (cc // 16)
    lane = lax.broadcasted_iota(jnp.int32, (CHUNK, LANE), 1)
    m0 = lane < RW_HEAD

    def stack(x):
        return jnp.concatenate([jnp.where(m0, x, 0.0), jnp.where(m0, 0.0, x)], axis=0)

    def body(c, carry):
        rows = pl.ds(pl.multiple_of(c * CHUNK, CHUNK), CHUNK)
        w_end, vv_b, stacks = [], [], []
        for b in range(nb):
            lw = lw_s[b, rows, :]
            cum = _sel_l(tri, lw)
            e_cum = jnp.exp(cum)
            e_inv = jnp.exp(-cum)
            ah = al_s[b, rows, :] * jnp.exp(cum - lw)
            bh = be_s[b, rows, :] * e_inv
            kh = k_s[b, rows, :] * e_inv
            rh = r_s[b, rows, :] * e_cum
            vv = v_s[b, rows, :]
            vv_b.append(vv)
            for j in range(NPAIR):
                sl = slice(j * LANE, (j + 1) * LANE)
                w_end.append(e_cum[CHUNK - 1:CHUNK, sl])
                stacks.append(tuple(stack(x[:, sl]) for x in (ah, rh, bh, kh, vv)))
        nch = len(stacks)
        ar = [jnp.concatenate([s[0], s[1]], axis=0) for s in stacks]
        bk = [jnp.concatenate([s[2], s[3]], axis=0) for s in stacks]
        big = [_mm_nt(x, y) for x, y in zip(ar, bk)]
        a_ab = [jnp.where(strict, x[0:C2, 0:C2], 0.0) for x in big]
        a_ak = [jnp.where(strict, x[0:C2, C2:], 0.0) for x in big]
        r_b = [jnp.where(incl, x[C2:, 0:C2], 0.0) for x in big]
        r_k = [jnp.where(incl, x[C2:, C2:], 0.0) for x in big]
        t_inv = _unit_lower_inverse(a_ab, eye, same_blk)
        mt = [mt_s[n] for n in range(nch)]
        rhs = [_mm_nt(s[0], m) + _mm(x, s[4]) for s, m, x in zip(stacks, mt, a_ak)]
        u_st = [_mm(x, y) for x, y in zip(t_inv, rhs)]
        y_st = [_mm_nt(s[1], m) + _mm(x, u) + _mm(z, s[4])
                for s, m, x, u, z in zip(stacks, mt, r_b, u_st, r_k)]
        upd = [_mm_tn(jnp.concatenate([u, s[4]], axis=0), y) for u, s, y in zip(u_st, stacks, bk)]
        for n in range(nch):
            mt_s[n] = (mt[n] + upd[n]) * w_end[n]
        for b in range(nb):
            y = jnp.concatenate([y_st[b * NPAIR + j][0:CHUNK] + y_st[b * NPAIR + j][CHUNK:]
                                 for j in range(NPAIR)], axis=1)
            mean = _sel_r(y, bd) * (1.0 / RW_HEAD)
            yc = y - mean
            var = _sel_r(yc * yc, bd) * (1.0 / RW_HEAD)
            yn = yc * lax.rsqrt(var + RW_GN_EPS) * nw_ref[...] + nb_ref[...]
            bonus = _sel_r(r_s[b, rows, :] * k_s[b, rows, :] * rk_ref[...], bd) * vv_b[b]
            o_ref[b, rows, :] = ((yn + bonus) * g_s[b, rows, :]).astype(o_ref.dtype)
        return carry

    lax.fori_loop(0, tc // CHUNK, body, 0)


def _rwkv(pr, mu, w0, w2pad, a0, a2pad, g2, k_k, k_a, r_k, norm_w, norm_b, bd, B, S, *, tc=256):
    T, N = pr.shape
    tc = min(tc, S)
    ns = S // tc
    R = RW_DIM
    const = lambda i: (0, 0)
    row = lambda x: x.reshape(1, -1)
    vec = pl.BlockSpec((1, R), const)
    out = pl.pallas_call(
        functools.partial(_rw_kernel, tc=tc, nb=B),
        grid=(ns,),
        in_specs=[
            pl.BlockSpec((B, tc, N), lambda i: (0, i, 0)),
            pl.BlockSpec((1, N), const),
            vec,
            pl.BlockSpec((LANE, R), const),
            vec,
            pl.BlockSpec((LANE, R), const),
            pl.BlockSpec((LANE, R), const),
            vec, vec, vec, vec, vec,
            pl.BlockSpec((R, R), const),
        ],
        out_specs=pl.BlockSpec((B, tc, R), lambda i: (0, i, 0)),
        out_shape=jax.ShapeDtypeStruct((B, S, R), MXU_DTYPE),
        scratch_shapes=[
            pltpu.VMEM((B * RW_HEADS // 2, LANE, LANE), F32),
            pltpu.VMEM((B, tc + SUBLANE, N), F32),
        ] + [pltpu.VMEM((B, tc, R), F32) for _ in range(7)],
        compiler_params=_cparams(("arbitrary",)),
        name="rwkv7",
    )(pr.reshape(B, S, N), row(mu), row(w0), w2pad, row(a0), a2pad, g2, row(k_k), row(k_a), row(r_k),
      row(norm_w), row(norm_b), bd)
    return out.reshape(T, R)


def _merge_kernel(h_ref, g_ref, wgate_ref, gb_ref, ya_ref, yb_ref, yc_ref, yd_ref, wb_ref, wo_ref, o_ref):
    h = h_ref[...]
    D = h.shape[1]
    xn = _rmsnorm_rows(h, g_ref[...]).astype(wgate_ref.dtype)
    merged = jnp.zeros(h.shape, F32)
    for n, y_ref in enumerate((ya_ref, yb_ref, yc_ref, yd_ref)):
        logits = jnp.dot(xn, wgate_ref[:, n * D:(n + 1) * D], preferred_element_type=F32) + gb_ref[n:n + 1, :]
        proj = jnp.dot(y_ref[...], wb_ref[n], preferred_element_type=F32)
        merged = merged + _sigmoid(logits) * proj
    o_ref[...] = h + jnp.dot(merged.astype(wo_ref.dtype), wo_ref[...], preferred_element_type=F32)


def _merge(h, g, wgate, gate_b, ys, w_branch, w_out, *, tm=256):
    T, D = h.shape
    tm = min(tm, T)
    NB, MW, _ = w_branch.shape
    const2 = lambda i: (0, 0)
    ytile = pl.BlockSpec((tm, MW), lambda i: (i, 0))
    return pl.pallas_call(
        _merge_kernel,
        grid=(T // tm,),
        in_specs=[
            pl.BlockSpec((tm, D), lambda i: (i, 0)),
            pl.BlockSpec((1, D), const2),
            pl.BlockSpec((D, NB * D), const2),
            pl.BlockSpec((NB, D), const2),
            ytile, ytile, ytile, ytile,
            pl.BlockSpec((NB, MW, D), lambda i: (0, 0, 0)),
            pl.BlockSpec((D, D), const2),
        ],
        out_specs=pl.BlockSpec((tm, D), lambda i: (i, 0)),
        out_shape=jax.ShapeDtypeStruct((T, D), F32),
        compiler_params=_cparams(("parallel",)),
        name="merge",
    )(h, g.reshape(1, D), wgate, gate_b, *ys, w_branch, w_out)


def _pad_cols(w, n):
    return jnp.pad(w, ((0, 0), (0, n - w.shape[1])))


def _pad_rows_at(w, start, total):
    return jnp.pad(w, ((start, total - start - w.shape[0]), (0, 0)))


def _rope_tables(S):
    half = DA_QK // 2
    inv_freq = ROPE_THETA ** (-jnp.arange(half, dtype=F32) / half)
    ang = jnp.arange(S, dtype=F32)[:, None] * inv_freq[None, :]
    cos, sin = jnp.cos(ang), jnp.sin(ang)
    reps = LANE // DA_QK
    return jnp.tile(jnp.concatenate([cos, cos], axis=1), (1, reps)), jnp.tile(jnp.concatenate([-sin, sin], axis=1), (1, reps))


def _head_expand(heads, width):
    r = jnp.arange(LANE)[:, None]
    c = jnp.arange(heads * width)[None, :] // width
    return (r == c).astype(F32)


def kernel(x, ffn1_norm, ffn1_wg, ffn1_wu, ffn1_wd, mix_norm, w_in, da_lambda_q1, da_lambda_k1, da_lambda_q2, da_lambda_k2, da_norm, gla_gate_w2, gla_gate_b, gla_norm, ssd_conv_w, ssd_conv_b, ssd_dt_bias, ssd_a_log, ssd_d, ssd_norm, rw_mu, rw_w0, rw_w2, rw_a0, rw_a2, rw_g2, rw_k_k, rw_k_a, rw_r_k, rw_norm_w, rw_norm_b, w_branch, gate_b, w_out, ffn2_norm, ffn2_wg, ffn2_wu, ffn2_wd, final_norm):
    B, S, D = x.shape
    depth = w_in.shape[0]
    T = B * S
    h = x.reshape(T, D)
    cos_t, sin_t = _rope_tables(S)
    e64 = _head_expand(SSD_HEADS, SSD_P)
    e128 = _head_expand(SSD_HEADS, LANE)
    rw_bd = (jnp.arange(RW_DIM)[:, None] // RW_HEAD == jnp.arange(RW_DIM)[None, :] // RW_HEAD).astype(F32)

    n_da = DA_HEADS * (4 * DA_QK + DA_V)
    n_gla = 2 * GLA_HEADS * GLA_DK + 2 * GLA_HEADS * GLA_DV + GLA_RANK
    n_ssd = SSD_INNER + (SSD_INNER + 2 * SSD_GROUPS * SSD_N) + SSD_HEADS
    n_rw = 3 * RW_DIM + rw_w2.shape[1] + rw_a2.shape[1] + rw_g2.shape[1]
    o_gla, o_ssd, o_rw, o_gate = n_da, n_da + n_gla, n_da + n_gla + n_ssd, n_da + n_gla + n_ssd + n_rw
    pad128 = lambda n: -(-n // LANE) * LANE
    cast = lambda w: w.astype(MXU_DTYPE)

    for l in range(depth):
        h = _ffn(h, ffn1_norm[l], cast(ffn1_wg[l]), cast(ffn1_wu[l]), cast(ffn1_wd[l]), final_norm, final=False)

        wl = w_in[l]
        w_da = cast(wl[:, 0:n_da])
        w_gla = cast(_pad_cols(wl[:, o_gla:o_ssd], pad128(n_gla)))
        w_ssd = cast(_pad_cols(wl[:, o_ssd:o_rw], pad128(n_ssd)))
        w_rw = cast(wl[:, o_rw:o_gate])
        w_gate = cast(wl[:, o_gate:])
        g = mix_norm[l]

        lam_init = 0.8 - 0.6 * math.exp(-0.3 * l)
        lam = (jnp.exp(jnp.sum(da_lambda_q1[l] * da_lambda_k1[l])) - jnp.exp(jnp.sum(da_lambda_q2[l] * da_lambda_k2[l]))
               + lam_init).reshape(1, 1).astype(F32)
        pda = _proj_da(h, g, w_da, cos_t, sin_t, S)
        y_a = _diff_attention(lam, pda, da_norm[l], B, S, lam_init)

        pg = _proj(h, g, w_gla, F32)
        y_b = _gla(pg, _pad_rows_at(gla_gate_w2[l], 0, LANE), gla_gate_b[l], gla_norm[l], B, S)

        ps = _proj(h, g, w_ssd, F32)
        dtb_pad = _pad_cols(ssd_dt_bias[l].reshape(1, -1), LANE)
        a_pad = _pad_cols(-jnp.exp(ssd_a_log[l]).reshape(1, -1), LANE)
        d_x = jnp.repeat(ssd_d[l], SSD_P).reshape(1, -1)
        y_c = _ssd(ps, ssd_conv_w[l], ssd_conv_b[l], dtb_pad, a_pad, e64, e128, d_x, ssd_norm[l], B, S)

        pr = _proj(h, g, w_rw, F32)
        rank_w = rw_w2.shape[1]
        y_d = _rwkv(pr, rw_mu[l], rw_w0[l], _pad_rows_at(rw_w2[l], 0, LANE), rw_a0[l],
                    _pad_rows_at(rw_a2[l], rank_w, LANE), cast(rw_g2[l]), rw_k_k[l], rw_k_a[l], rw_r_k[l],
                    rw_norm_w[l], rw_norm_b[l], rw_bd, B, S)

        h = _merge(h, g, w_gate, gate_b[l], (y_a, y_b, y_c, y_d), cast(w_branch[l]), cast(w_out[l]))
        h = _ffn(h, ffn2_norm[l], cast(ffn2_wg[l]), cast(ffn2_wu[l]), cast(ffn2_wd[l]), final_norm,
                 final=(l == depth - 1))
    return h.reshape(B, S, D)
```

```python
import functools
import math

import jax
import jax.numpy as jnp
from jax import lax
from jax.experimental import pallas as pl
from jax.experimental.pallas import tpu as pltpu

F32 = jnp.float32
BF16 = jnp.bfloat16
MXU_DTYPE = BF16

CHUNK = 64
ROPE_THETA = 10000.0
EPS = 1e-6
DA_HEADS, DA_QK, DA_V = 4, 64, 128
GLA_HEADS, GLA_DK, GLA_DV, GLA_RANK, GLA_TAU = 4, 64, 128, 16, 16.0
SSD_HEADS, SSD_P, SSD_GROUPS, SSD_N, SSD_CONV = 8, 64, 2, 128, 4
SSD_INNER = SSD_HEADS * SSD_P
RW_HEADS, RW_HEAD = 8, 64
RW_DIM = RW_HEADS * RW_HEAD
RW_DECAY_SCALE = 0.606531
RW_GN_EPS = 64e-5
LANE = 128
SUBLANE = 8
VMEM_LIMIT = 48 * 1024 * 1024


def _cparams(sem):
    return pltpu.CompilerParams(dimension_semantics=sem, vmem_limit_bytes=VMEM_LIMIT)


def _mm(a, b):
    return jnp.dot(a.astype(MXU_DTYPE), b.astype(MXU_DTYPE), preferred_element_type=F32)


def _mm_nt(a, b):
    return lax.dot_general(a.astype(MXU_DTYPE), b.astype(MXU_DTYPE), (((1,), (1,)), ((), ())),
                           preferred_element_type=F32)


def _mm_tn(a, b):
    return lax.dot_general(a.astype(MXU_DTYPE), b.astype(MXU_DTYPE), (((0,), (0,)), ((), ())),
                           preferred_element_type=F32)


def _split(x):
    hi = x.astype(MXU_DTYPE)
    lo = (x - hi.astype(F32)).astype(MXU_DTYPE)
    return hi, lo


def _sel_l(m01, x):
    hi, lo = _split(x)
    m = m01.astype(MXU_DTYPE)
    return jnp.dot(m, hi, preferred_element_type=F32) + jnp.dot(m, lo, preferred_element_type=F32)


def _sel_r(x, m01):
    hi, lo = _split(x)
    m = m01.astype(MXU_DTYPE)
    return jnp.dot(hi, m, preferred_element_type=F32) + jnp.dot(lo, m, preferred_element_type=F32)


def _sigmoid(x):
    return 1.0 / (1.0 + jnp.exp(-x))


def _silu(x):
    return x * _sigmoid(x)


def _softplus(x):
    return jnp.maximum(x, 0.0) + jnp.log1p(jnp.exp(-jnp.abs(x)))


def _rmsnorm_rows(x, g):
    return x * lax.rsqrt(jnp.mean(x * x, axis=-1, keepdims=True) + EPS) * g


def _tri(n, strict=False):
    r = lax.broadcasted_iota(jnp.int32, (n, n), 0)
    c = lax.broadcasted_iota(jnp.int32, (n, n), 1)
    return (c < r) if strict else (c <= r)


def _ffn_kernel(h_ref, g_ref, wg_ref, wu_ref, wd_ref, fg_ref, o_ref, xn_s, act_s, *, tf, final):
    xn_s[...] = _rmsnorm_rows(h_ref[...], g_ref[...]).astype(xn_s.dtype)
    F = wg_ref.shape[1]
    for f in range(F // tf):
        cols = slice(f * tf, (f + 1) * tf)
        xn = xn_s[...]
        gate = jnp.dot(xn, wg_ref[:, cols], preferred_element_type=F32)
        up = jnp.dot(xn, wu_ref[:, cols], preferred_element_type=F32)
        act_s[:, cols] = (_silu(gate) * up).astype(act_s.dtype)
    out = h_ref[...] + 0.5 * jnp.dot(act_s[...], wd_ref[...], preferred_element_type=F32)
    if final:
        out = _rmsnorm_rows(out, fg_ref[...])
    o_ref[...] = out


def _resident(shape):
    return pl.BlockSpec(shape, lambda *_: (0,) * len(shape), pipeline_mode=pl.Buffered(1))


def _ffn(h, g, wg, wu, wd, fg, *, final, tm=256, tf=256):
    T, D = h.shape
    F = wg.shape[1]
    tm = min(tm, T)
    return pl.pallas_call(
        functools.partial(_ffn_kernel, tf=tf, final=final),
        grid=(T // tm,),
        in_specs=[
            pl.BlockSpec((tm, D), lambda i: (i, 0)),
            _resident((1, D)),
            _resident((D, F)),
            _resident((D, F)),
            _resident((F, D)),
            _resident((1, D)),
        ],
        out_specs=pl.BlockSpec((tm, D), lambda i: (i, 0)),
        out_shape=jax.ShapeDtypeStruct((T, D), F32),
        scratch_shapes=[pltpu.VMEM((tm, D), MXU_DTYPE), pltpu.VMEM((tm, F), MXU_DTYPE)],
        compiler_params=_cparams(("parallel",)),
        name="ffn",
    )(h, g.reshape(1, D), wg, wu, wd, fg.reshape(1, D))


def _proj_kernel(h_ref, g_ref, w_ref, o_ref):
    xn = _rmsnorm_rows(h_ref[...], g_ref[...]).astype(w_ref.dtype)
    o_ref[...] = jnp.dot(xn, w_ref[...], preferred_element_type=F32).astype(o_ref.dtype)


def _proj(h, g, w, out_dtype, *, tm=512):
    T, D = h.shape
    N = w.shape[1]
    tm = min(tm, T)
    return pl.pallas_call(
        _proj_kernel,
        grid=(T // tm,),
        in_specs=[
            pl.BlockSpec((tm, D), lambda i: (i, 0)),
            pl.BlockSpec((1, D), lambda i: (0, 0)),
            pl.BlockSpec((D, N), lambda i: (0, 0)),
        ],
        out_specs=pl.BlockSpec((tm, N), lambda i: (i, 0)),
        out_shape=jax.ShapeDtypeStruct((T, N), out_dtype),
        compiler_params=_cparams(("parallel",)),
        name="proj",
    )(h, g.reshape(1, D), w)


def _proj_da_kernel(h_ref, g_ref, w_ref, cos_ref, sin_ref, o_ref, *, nqk):
    xn = _rmsnorm_rows(h_ref[...], g_ref[...]).astype(w_ref.dtype)
    p = jnp.dot(xn, w_ref[...], preferred_element_type=F32)
    cos = cos_ref[...]
    sin = sin_ref[...]
    tm = p.shape[0]
    lane = lax.broadcasted_iota(jnp.int32, (tm, LANE), 1)
    first_half = (lane % DA_QK) < (DA_QK // 2)
    qscale = DA_QK ** -0.5 * math.log2(math.e)
    for blk in range(2 * nqk):
        t = p[:, blk * LANE:(blk + 1) * LANE]
        swapped = jnp.where(first_half, pltpu.roll(t, LANE - DA_QK // 2, 1), pltpu.roll(t, DA_QK // 2, 1))
        t = t * cos + swapped * sin
        if blk < nqk:
            t = t * qscale
        o_ref[:, blk * LANE:(blk + 1) * LANE] = t.astype(o_ref.dtype)
    o_ref[:, 2 * nqk * LANE:] = p[:, 2 * nqk * LANE:].astype(o_ref.dtype)


def _proj_da(h, g, w, cos_t, sin_t, S, *, tm=512):
    T, D = h.shape
    N = w.shape[1]
    tm = min(tm, S)
    ns = S // tm
    nqk = DA_HEADS * 2 * DA_QK // LANE
    return pl.pallas_call(
        functools.partial(_proj_da_kernel, nqk=nqk),
        grid=(T // tm,),
        in_specs=[
            pl.BlockSpec((tm, D), lambda i: (i, 0)),
            pl.BlockSpec((1, D), lambda i: (0, 0)),
            pl.BlockSpec((D, N), lambda i: (0, 0)),
            pl.BlockSpec((tm, LANE), lambda i: (i % ns, 0)),
            pl.BlockSpec((tm, LANE), lambda i: (i % ns, 0)),
        ],
        out_specs=pl.BlockSpec((tm, N), lambda i: (i, 0)),
        out_shape=jax.ShapeDtypeStruct((T, N), MXU_DTYPE),
        compiler_params=_cparams(("parallel",)),
        name="proj_da",
    )(h, g.reshape(1, D), w, cos_t, sin_t)


def _da_kernel(lam_ref, q_ref, k_ref, v_ref, g_ref, o_ref, q_s, m_s, l_s, acc_s, *, tq, nsub, out_scale):
    qi = pl.program_id(2)
    q = q_ref[...]
    lane = lax.broadcasted_iota(jnp.int32, q.shape, 1)
    zero = jnp.zeros_like(q)
    q_s[0] = jnp.where(lane < DA_QK, q, zero)
    q_s[1] = jnp.where(lane >= DA_QK, q, zero)
    m_s[...] = jnp.full_like(m_s, -jnp.inf)
    l_s[...] = jnp.zeros_like(l_s)
    acc_s[...] = jnp.zeros_like(acc_s)
    tr = tq // nsub
    reps = tq // LANE
    chains = [(c, pl.ds(r * tr, tr), r) for r in range(nsub) for c in range(2)]

    def step(j, masked):
        rows = pl.ds(pl.multiple_of(j * tq, tq), tq)
        k = k_ref[rows, :]
        v = v_ref[rows, :]
        s = [lax.dot_general(q_s[c, rs, :], k, (((1,), (1,)), ((), ())), preferred_element_type=F32)
             for c, rs, _ in chains]
        if masked:
            col = lax.broadcasted_iota(jnp.int32, (tr, tq), 1) // CHUNK
            row = lax.broadcasted_iota(jnp.int32, (tr, tq), 0)
            s = [jnp.where(col <= (row + r * tr) // CHUNK, x, -jnp.inf) for x, (_, _, r) in zip(s, chains)]
        m_prev = [m_s[c, rs, :] for c, rs, _ in chains]
        m_new = [jnp.maximum(mp, jnp.max(x, axis=-1, keepdims=True)) for mp, x in zip(m_prev, s)]
        p = [jnp.exp2(x - jnp.concatenate([mn] * reps, axis=1)) for x, mn in zip(s, m_new)]
        alpha = [jnp.exp2(mp - mn) for mp, mn in zip(m_prev, m_new)]
        pv = [jnp.dot(x.astype(v.dtype), v, preferred_element_type=F32) for x in p]
        for (c, rs, _), mn, a, x, o in zip(chains, m_new, alpha, p, pv):
            m_s[c, rs, :] = mn
            l_s[c, rs, :] = a * l_s[c, rs, :] + jnp.sum(x, axis=-1, keepdims=True)
            acc_s[c, rs, :] = a * acc_s[c, rs, :] + o

    def full_block(j, carry):
        step(j, False)
        return carry

    lax.fori_loop(0, qi, full_block, 0)
    step(qi, True)
    lam = lam_ref[0, 0]
    o = acc_s[0] / l_s[0] - lam * (acc_s[1] / l_s[1])
    o = _rmsnorm_rows(o, g_ref[...]) * out_scale
    o_ref[...] = o.astype(o_ref.dtype)


def _diff_attention(lam, pda, da_norm, B, S, lam_init, *, tq=512, nsub=2):
    T = pda.shape[0]
    tq = min(tq, S)
    nq = S // tq
    H = DA_HEADS
    kernel = functools.partial(_da_kernel, tq=tq, nsub=nsub, out_scale=1.0 - lam_init)
    return pl.pallas_call(
        kernel,
        grid=(B, H, nq),
        in_specs=[
            pl.BlockSpec(memory_space=pltpu.SMEM),
            pl.BlockSpec((tq, LANE), lambda b, h, i: (b * nq + i, h)),
            pl.BlockSpec((S, LANE), lambda b, h, i: (b, H + h)),
            pl.BlockSpec((S, LANE), lambda b, h, i: (b, 2 * H + h)),
            pl.BlockSpec((1, LANE), lambda b, h, i: (0, 0)),
        ],
        out_specs=pl.BlockSpec((tq, LANE), lambda b, h, i: (b * nq + i, h)),
        out_shape=jax.ShapeDtypeStruct((T, H * DA_V), MXU_DTYPE),
        scratch_shapes=[
            pltpu.VMEM((2, tq, LANE), MXU_DTYPE),
            pltpu.VMEM((2, tq, LANE), F32),
            pltpu.VMEM((2, tq, LANE), F32),
            pltpu.VMEM((2, tq, DA_V), F32),
        ],
        compiler_params=_cparams(("parallel", "parallel", "arbitrary")),
        name="diff_attn",
    )(lam, pda, pda, pda, da_norm.reshape(1, DA_V))


def _gla_kernel(p_ref, w2_ref, gb_ref, ng_ref, o_ref, st_s, lg_s, *, tc):
    i = pl.program_id(1)
    HK = GLA_HEADS * GLA_DK
    HV = GLA_HEADS * GLA_DV

    @pl.when(i == 0)
    def _():
        st_s[...] = jnp.zeros_like(st_s)

    glr = p_ref[:, 2 * HK + 2 * HV:]
    z = _mm(glr, w2_ref[...]) + gb_ref[...]
    lg_s[...] = (jnp.minimum(z, 0.0) - jnp.log1p(jnp.exp(-jnp.abs(z)))) * (1.0 / GLA_TAU)

    tri = _tri(CHUNK).astype(F32)
    causal = _tri(CHUNK)
    lane_k = lax.broadcasted_iota(jnp.int32, (CHUNK, HK), 1) // GLA_DK
    bd = (lax.broadcasted_iota(jnp.int32, (HV, HK), 0) // GLA_DV
          == lax.broadcasted_iota(jnp.int32, (HV, HK), 1) // GLA_DK)
    ng = ng_ref[...]

    def body(c, carry):
        rows = pl.ds(pl.multiple_of(c * CHUNK, CHUNK), CHUNK)
        q = p_ref[rows, 0:HK] * (GLA_DK ** -0.5)
        k = p_ref[rows, HK:2 * HK]
        v = p_ref[rows, 2 * HK:2 * HK + HV]
        og = p_ref[rows, 2 * HK + HV:2 * HK + 2 * HV]
        G = _sel_l(tri, lg_s[rows, :])
        g_ref_row = G[CHUNK // 2:CHUNK // 2 + 1]
        g_last = G[CHUNK - 1:CHUNK]
        qe = q * jnp.exp(G - g_ref_row)
        ke = k * jnp.exp(g_ref_row - G)
        q_in = q * jnp.exp(G)
        k_end = k * jnp.exp(g_last - G)
        st = st_s[...]
        y_inter = _mm_nt(q_in, st)
        outs = []
        for h in range(GLA_HEADS):
            att = _mm_nt(jnp.where(lane_k == h, qe, 0.0), ke)
            att = jnp.where(causal, att, 0.0)
            vh = v[:, h * GLA_DV:(h + 1) * GLA_DV]
            oh = _mm(att, vh) + y_inter[:, h * GLA_DV:(h + 1) * GLA_DV]
            oh = _rmsnorm_rows(oh, ng)
            outs.append(oh)
        o = jnp.concatenate(outs, axis=1) * _silu(og)
        o_ref[rows, :] = o.astype(o_ref.dtype)
        kv = _mm_tn(v, k_end)
        st_s[...] = st * jnp.exp(g_last) + jnp.where(bd, kv, 0.0)
        return carry

    lax.fori_loop(0, tc // CHUNK, body, 0)


def _gla(pg, w2pad, gate_b, gla_norm, B, S, *, tc=512):
    T, N = pg.shape
    tc = min(tc, S)
    ns = S // tc
    HK, HV = GLA_HEADS * GLA_DK, GLA_HEADS * GLA_DV
    return pl.pallas_call(
        functools.partial(_gla_kernel, tc=tc),
        grid=(B, ns),
        in_specs=[
            pl.BlockSpec((tc, N), lambda b, i: (b * ns + i, 0)),
            pl.BlockSpec((LANE, HK), lambda b, i: (0, 0)),
            pl.BlockSpec((1, HK), lambda b, i: (0, 0)),
            pl.BlockSpec((1, GLA_DV), lambda b, i: (0, 0)),
        ],
        out_specs=pl.BlockSpec((tc, HV), lambda b, i: (b * ns + i, 0)),
        out_shape=jax.ShapeDtypeStruct((T, HV), MXU_DTYPE),
        scratch_shapes=[pltpu.VMEM((HV, HK), F32), pltpu.VMEM((tc, HK), F32)],
        compiler_params=_cparams(("parallel", "arbitrary")),
        name="gla",
    )(pg, w2pad, gate_b.reshape(1, HK), gla_norm.reshape(1, GLA_DV))


def _ssd_kernel(p_ref, cw_ref, cb_ref, dtb_ref, a_ref, e64_ref, e128_ref, dx_ref, ng_ref, o_ref,
                st_s, xpad_s, xs_s, bc_s, ax_s, ax2_s, xdt_s, *, tc):
    i = pl.program_id(1)
    NI = SSD_INNER
    GN = SSD_GROUPS * SSD_N
    CD = NI + 2 * GN
    PAD = SUBLANE

    @pl.when(i == 0)
    def _():
        st_s[...] = jnp.zeros_like(st_s)
        xpad_s[0:PAD, :] = jnp.zeros((PAD, CD), F32)

    xpad_s[PAD:PAD + tc, :] = p_ref[:, NI:NI + CD]
    conv = cb_ref[...]
    for kk in range(SSD_CONV):
        off = PAD - (SSD_CONV - 1) + kk
        conv = conv + cw_ref[kk:kk + 1, :] * xpad_s[off:off + tc, :]
    xpad_s[0:PAD, :] = xpad_s[tc:tc + PAD, :]
    xbc = _silu(conv)
    xs_s[...] = xbc[:, 0:NI]
    bc_s[...] = xbc[:, NI:]
    dt = _softplus(p_ref[:, NI + CD:] + dtb_ref[...])
    dta = dt * a_ref[...]
    ax_s[...] = _sel_r(dta, e64_ref[...])
    ax2_s[...] = _sel_r(dta, e128_ref[...])
    xdt_s[...] = xbc[:, 0:NI] * _sel_r(dt, e64_ref[...])

    tri = _tri(CHUNK).astype(F32)
    tri_t = (lax.broadcasted_iota(jnp.int32, (CHUNK, CHUNK), 0)
             <= lax.broadcasted_iota(jnp.int32, (CHUNK, CHUNK), 1)).astype(F32)
    causal = _tri(CHUNK)
    low_half = lax.broadcasted_iota(jnp.int32, (CHUNK, LANE), 1) < SSD_P
    rep = SSD_HEADS // SSD_GROUPS
    GW = rep * SSD_P

    def body(c, carry):
        rows = pl.ds(pl.multiple_of(c * CHUNK, CHUNK), CHUNK)
        xdt = xdt_s[rows, :]
        acx = _sel_l(tri, ax_s[rows, :])
        acx2 = _sel_l(tri, ax2_s[rows, :])
        dta2 = ax2_s[rows, :]
        a_last = acx[CHUNK - 1:CHUNK]
        x_end = xdt * jnp.exp(a_last - acx)
        st = st_s[...]
        y_parts = []
        inter = []
        new_st = []
        for g in range(SSD_GROUPS):
            bm = bc_s[rows, g * SSD_N:(g + 1) * SSD_N]
            cm = bc_s[rows, GN + g * SSD_N:GN + (g + 1) * SSD_N]
            cb = _mm_nt(cm, bm)
            inter.append(_mm(cm, st[:, g * GW:(g + 1) * GW]))
            for pair in range(rep // 2):
                res = []
                for hh in range(2):
                    h = g * rep + pair * 2 + hh
                    col = acx2[:, h * LANE:h * LANE + CHUNK]
                    row = jnp.sum(tri_t * dta2[:, h * LANE:h * LANE + CHUNK], axis=0, keepdims=True)
                    lmat = jnp.exp(jnp.where(causal, col - row, -jnp.inf))
                    blk = (h // 2) * LANE
                    res.append(_mm(cb * lmat, xdt[:, blk:blk + LANE]))
                y_parts.append(jnp.where(low_half, res[0], res[1]))
            new_st.append(_mm_tn(bm, x_end[:, g * GW:(g + 1) * GW]))
        y_intra = jnp.concatenate(y_parts, axis=1)
        y_inter_all = jnp.concatenate(inter, axis=1) * jnp.exp(acx)
        st_s[...] = st * jnp.exp(a_last) + jnp.concatenate(new_st, axis=1)
        y = y_intra + y_inter_all + dx_ref[...] * xs_s[rows, :]
        y = y * _silu(p_ref[rows, 0:NI])
        half = NI // SSD_GROUPS
        outs = [_rmsnorm_rows(y[:, g * half:(g + 1) * half], ng_ref[:, g * half:(g + 1) * half])
                for g in range(SSD_GROUPS)]
        o_ref[rows, :] = jnp.concatenate(outs, axis=1).astype(o_ref.dtype)
        return carry

    lax.fori_loop(0, tc // CHUNK, body, 0)


def _ssd(ps, conv_w, conv_b, dtb_pad, a_pad, e64, e128, d_x, ssd_norm, B, S, *, tc=512):
    T, N = ps.shape
    tc = min(tc, S)
    ns = S // tc
    NI = SSD_INNER
    CD = NI + 2 * SSD_GROUPS * SSD_N
    const = lambda b, i: (0, 0)
    return pl.pallas_call(
        functools.partial(_ssd_kernel, tc=tc),
        grid=(B, ns),
        in_specs=[
            pl.BlockSpec((tc, N), lambda b, i: (b * ns + i, 0)),
            pl.BlockSpec((SSD_CONV, CD), const),
            pl.BlockSpec((1, CD), const),
            pl.BlockSpec((1, LANE), const),
            pl.BlockSpec((1, LANE), const),
            pl.BlockSpec((LANE, NI), const),
            pl.BlockSpec((LANE, SSD_HEADS * LANE), const),
            pl.BlockSpec((1, NI), const),
            pl.BlockSpec((1, NI), const),
        ],
        out_specs=pl.BlockSpec((tc, NI), lambda b, i: (b * ns + i, 0)),
        out_shape=jax.ShapeDtypeStruct((T, NI), MXU_DTYPE),
        scratch_shapes=[
            pltpu.VMEM((SSD_N, NI), F32),
            pltpu.VMEM((tc + 2 * SUBLANE, CD), F32),
            pltpu.VMEM((tc, NI), F32),
            pltpu.VMEM((tc, 2 * SSD_GROUPS * SSD_N), F32),
            pltpu.VMEM((tc, NI), F32),
            pltpu.VMEM((tc, SSD_HEADS * LANE), F32),
            pltpu.VMEM((tc, NI), F32),
        ],
        compiler_params=_cparams(("parallel", "arbitrary")),
        name="ssd",
    )(ps, conv_w, conv_b.reshape(1, CD), dtb_pad, a_pad, e64, e128, d_x, ssd_norm.reshape(1, NI))


def _unit_lower_inverse(mats, eye, same_blk, stack):
    d = [jnp.where(same_blk, a, 0.0) for a in mats]
    o = [a - x for a, x in zip(mats, d)]
    t = [eye + x for x in d]
    dp = d
    sd = [stack(x) for x in dp]
    for _ in range(3):
        dp = [_mm(x, y) for x, y in zip(dp, sd)]
        sd = [stack(x) for x in dp]
        t = [x + _mm(x, y) for x, y in zip(t, sd)]
    n = [_mm(x, stack(y)) for x, y in zip(t, o)]
    n2 = [_mm(x, stack(x)) for x in n]
    q = [eye + x for x in n]
    q = [x + _mm(x, stack(y)) for x, y in zip(q, n2)]
    return [_mm(x, stack(y)) for x, y in zip(q, t)]


def _rw_kernel(p_ref, mu_ref, w0_ref, w2_ref, a0_ref, a2_ref, g2_ref, kk_ref, ka_ref, rk_ref,
               nw_ref, nb_ref, bd_ref, o_ref,
               mt_s, ppad_s, r_s, k_s, v_s, lw_s, al_s, be_s, g_s, *, tc, nb, ahead):
    i = pl.program_id(0)
    R = RW_DIM
    PAD = SUBLANE
    NP = p_ref.shape[2]
    NPAIR = RW_HEADS // 2

    @pl.when(i == 0)
    def _():
        mt_s[...] = jnp.zeros_like(mt_s)
        ppad_s[:, 0:PAD, :] = jnp.zeros((nb, PAD, NP), F32)

    bd = bd_ref[...]
    for b in range(nb):
        p = p_ref[b]
        ppad_s[b, PAD:PAD + tc, :] = p
        prev = ppad_s[b, PAD - 1:PAD - 1 + tc, :]
        ppad_s[b, 0:PAD, :] = ppad_s[b, tc:tc + PAD, :]
        pm = p + (prev - p) * mu_ref[...]
        r = pm[:, 0:R]
        k = pm[:, R:2 * R]
        v = pm[:, 2 * R:3 * R]
        wa = pm[:, 3 * R:3 * R + LANE]
        gl = pm[:, 3 * R + LANE:3 * R + 2 * LANE]
        lw_s[b] = -RW_DECAY_SCALE * _sigmoid(w0_ref[...] + _mm(jnp.tanh(wa), w2_ref[...]))
        a = _sigmoid(a0_ref[...] + _mm(wa, a2_ref[...]))
        g_s[b] = _mm(_sigmoid(gl), g2_ref[...])
        kk = k * kk_ref[...]
        kk = kk / jnp.maximum(jnp.sqrt(_sel_r(kk * kk, bd)), 1e-12)
        r_s[b] = r
        k_s[b] = k * (1.0 + (a - 1.0) * ka_ref[...])
        v_s[b] = v
        al_s[b] = -kk
        be_s[b] = kk * a

    C2 = 2 * CHUNK
    tri = _tri(CHUNK).astype(F32)
    tt = lax.broadcasted_iota(jnp.int32, (CHUNK, C2), 0)
    ss = lax.broadcasted_iota(jnp.int32, (CHUNK, C2), 1) % CHUNK
    strict = ss < tt
    incl = ss <= tt
    eye = (ss == tt).astype(F32)
    same_blk = (tt // 16) == (ss // 16)
    m0 = lax.broadcasted_iota(jnp.int32, (CHUNK, LANE), 1) < RW_HEAD
    bdm = (lax.broadcasted_iota(jnp.int32, (C2, C2), 0) // CHUNK
           == lax.broadcasted_iota(jnp.int32, (C2, C2), 1) // CHUNK)

    def stack(x):
        xb = x.astype(MXU_DTYPE)
        zero = jnp.zeros_like(xb)
        return jnp.concatenate([jnp.where(m0, xb, zero), jnp.where(m0, zero, xb)], axis=0)

    def body(c, carry):
        row_sl = [pl.ds(pl.multiple_of((c * ahead + u) * CHUNK, CHUNK), CHUNK) for u in range(ahead)]
        w_end, vv_b, wide = [], [], []
        for rows in row_sl:
            for b in range(nb):
                lw = lw_s[b, rows, :]
                cum = _sel_l(tri, lw)
                e_cum = jnp.exp(cum)
                e_inv = jnp.exp(-cum)
                ah = al_s[b, rows, :] * jnp.exp(cum - lw)
                bh = be_s[b, rows, :] * e_inv
                kh = k_s[b, rows, :] * e_inv
                rh = r_s[b, rows, :] * e_cum
                vv = v_s[b, rows, :]
                vv_b.append(vv)
                for j in range(NPAIR):
                    sl = slice(j * LANE, (j + 1) * LANE)
                    w_end.append(e_cum[CHUNK - 1:CHUNK, sl])
                    wide.append(tuple(x[:, sl].astype(MXU_DTYPE) for x in (ah, rh, bh, kh, vv)))
        nch = nb * NPAIR
        ar = [jnp.concatenate([w[0], w[1]], axis=0) for w in wide]
        bk = [jnp.concatenate([stack(w[2]), stack(w[3])], axis=0) for w in wide]
        v_bd = [stack(w[4]) for w in wide]
        big = [_mm_nt(x, y) for x, y in zip(ar, bk)]
        a_ab = [jnp.where(strict, x[0:CHUNK, 0:C2], 0.0) for x in big]
        a_ak = [jnp.where(strict, x[0:CHUNK, C2:], 0.0) for x in big]
        r_b = [jnp.where(incl, x[CHUNK:, 0:C2], 0.0) for x in big]
        r_k = [jnp.where(incl, x[CHUNK:, C2:], 0.0) for x in big]
        t_inv = _unit_lower_inverse(a_ab, eye, same_blk, stack)
        for u, rows in enumerate(row_sl):
            ch = slice(u * nch, (u + 1) * nch)
            mt = [mt_s[n] for n in range(nch)]
            rhs = [_mm_nt(w[0], m) + _mm(x, v) for w, m, x, v in zip(wide[ch], mt, a_ak[ch], v_bd[ch])]
            u_w = [_mm(x, stack(y)) for x, y in zip(t_inv[ch], rhs)]
            upd = [_mm_tn(jnp.concatenate([x.astype(MXU_DTYPE), w[4]], axis=0),
                          jnp.concatenate([w[2], w[3]], axis=0)) for x, w in zip(u_w, wide[ch])]
            for n in range(nch):
                mt_s[n] = (mt[n] + jnp.where(bdm, upd[n], 0.0)) * w_end[u * nch + n]
            y_w = [_mm_nt(w[1], m) + _mm(x, stack(uu)) + _mm(z, v)
                   for w, m, x, uu, z, v in zip(wide[ch], mt, r_b[ch], u_w, r_k[ch], v_bd[ch])]
            for b in range(nb):
                y = jnp.concatenate([y_w[b * NPAIR + j] for j in range(NPAIR)], axis=1)
                mean = _sel_r(y, bd) * (1.0 / RW_HEAD)
                yc = y - mean
                var = _sel_r(yc * yc, bd) * (1.0 / RW_HEAD)
                yn = yc * lax.rsqrt(var + RW_GN_EPS) * nw_ref[...] + nb_ref[...]
                bonus = _sel_r(r_s[b, rows, :] * k_s[b, rows, :] * rk_ref[...], bd) * vv_b[u * nb + b]
                o_ref[b, rows, :] = ((yn + bonus) * g_s[b, rows, :]).astype(o_ref.dtype)
        return carry

    lax.fori_loop(0, tc // (CHUNK * ahead), body, 0)


def _rwkv(pr, mu, w0, w2pad, a0, a2pad, g2, k_k, k_a, r_k, norm_w, norm_b, bd, B, S, *, tc=256, ahead=2):
    T, N = pr.shape
    tc = min(tc, S)
    ns = S // tc
    R = RW_DIM
    const = lambda i: (0, 0)
    row = lambda x: x.reshape(1, -1)
    vec = pl.BlockSpec((1, R), const)
    out = pl.pallas_call(
        functools.partial(_rw_kernel, tc=tc, nb=B, ahead=ahead),
        grid=(ns,),
        in_specs=[
            pl.BlockSpec((B, tc, N), lambda i: (0, i, 0)),
            pl.BlockSpec((1, N), const),
            vec,
            pl.BlockSpec((LANE, R), const),
            vec,
            pl.BlockSpec((LANE, R), const),
            pl.BlockSpec((LANE, R), const),
            vec, vec, vec, vec, vec,
            pl.BlockSpec((R, R), const),
        ],
        out_specs=pl.BlockSpec((B, tc, R), lambda i: (0, i, 0)),
        out_shape=jax.ShapeDtypeStruct((B, S, R), MXU_DTYPE),
        scratch_shapes=[
            pltpu.VMEM((B * RW_HEADS // 2, LANE, LANE), F32),
            pltpu.VMEM((B, tc + SUBLANE, N), F32),
        ] + [pltpu.VMEM((B, tc, R), F32) for _ in range(7)],
        compiler_params=_cparams(("arbitrary",)),
        name="rwkv7",
    )(pr.reshape(B, S, N), row(mu), row(w0), w2pad, row(a0), a2pad, g2, row(k_k), row(k_a), row(r_k),
      row(norm_w), row(norm_b), bd)
    return out.reshape(T, R)


def _merge_kernel(h_ref, g_ref, wgate_ref, gb_ref, ya_ref, yb_ref, yc_ref, yd_ref, wb_ref, wo_ref, o_ref):
    h = h_ref[...]
    D = h.shape[1]
    xn = _rmsnorm_rows(h, g_ref[...]).astype(wgate_ref.dtype)
    merged = jnp.zeros(h.shape, F32)
    for n, y_ref in enumerate((ya_ref, yb_ref, yc_ref, yd_ref)):
        logits = jnp.dot(xn, wgate_ref[:, n * D:(n + 1) * D], preferred_element_type=F32) + gb_ref[n:n + 1, :]
        proj = jnp.dot(y_ref[...], wb_ref[n], preferred_element_type=F32)
        merged = merged + _sigmoid(logits) * proj
    o_ref[...] = h + jnp.dot(merged.astype(wo_ref.dtype), wo_ref[...], preferred_element_type=F32)


def _merge(h, g, wgate, gate_b, ys, w_branch, w_out, *, tm=256):
    T, D = h.shape
    tm = min(tm, T)
    NB, MW, _ = w_branch.shape
    const2 = lambda i: (0, 0)
    ytile = pl.BlockSpec((tm, MW), lambda i: (i, 0))
    return pl.pallas_call(
        _merge_kernel,
        grid=(T // tm,),
        in_specs=[
            pl.BlockSpec((tm, D), lambda i: (i, 0)),
            pl.BlockSpec((1, D), const2),
            pl.BlockSpec((D, NB * D), const2),
            pl.BlockSpec((NB, D), const2),
            ytile, ytile, ytile, ytile,
            pl.BlockSpec((NB, MW, D), lambda i: (0, 0, 0)),
            pl.BlockSpec((D, D), const2),
        ],
        out_specs=pl.BlockSpec((tm, D), lambda i: (i, 0)),
        out_shape=jax.ShapeDtypeStruct((T, D), F32),
        compiler_params=_cparams(("parallel",)),
        name="merge",
    )(h, g.reshape(1, D), wgate, gate_b, *ys, w_branch, w_out)


def _pad_cols(w, n):
    return jnp.pad(w, ((0, 0), (0, n - w.shape[1])))


def _pad_rows_at(w, start, total):
    return jnp.pad(w, ((start, total - start - w.shape[0]), (0, 0)))


def _rope_tables(S):
    half = DA_QK // 2
    inv_freq = ROPE_THETA ** (-jnp.arange(half, dtype=F32) / half)
    ang = jnp.arange(S, dtype=F32)[:, None] * inv_freq[None, :]
    cos, sin = jnp.cos(ang), jnp.sin(ang)
    reps = LANE // DA_QK
    return jnp.tile(jnp.concatenate([cos, cos], axis=1), (1, reps)), jnp.tile(jnp.concatenate([-sin, sin], axis=1), (1, reps))


def _head_expand(heads, width):
    r = jnp.arange(LANE)[:, None]
    c = jnp.arange(heads * width)[None, :] // width
    return (r == c).astype(F32)


def kernel(x, ffn1_norm, ffn1_wg, ffn1_wu, ffn1_wd, mix_norm, w_in, da_lambda_q1, da_lambda_k1, da_lambda_q2, da_lambda_k2, da_norm, gla_gate_w2, gla_gate_b, gla_norm, ssd_conv_w, ssd_conv_b, ssd_dt_bias, ssd_a_log, ssd_d, ssd_norm, rw_mu, rw_w0, rw_w2, rw_a0, rw_a2, rw_g2, rw_k_k, rw_k_a, rw_r_k, rw_norm_w, rw_norm_b, w_branch, gate_b, w_out, ffn2_norm, ffn2_wg, ffn2_wu, ffn2_wd, final_norm):
    B, S, D = x.shape
    depth = w_in.shape[0]
    T = B * S
    h = x.reshape(T, D)
    cos_t, sin_t = _rope_tables(S)
    e64 = _head_expand(SSD_HEADS, SSD_P)
    e128 = _head_expand(SSD_HEADS, LANE)
    rw_bd = (jnp.arange(RW_DIM)[:, None] // RW_HEAD == jnp.arange(RW_DIM)[None, :] // RW_HEAD).astype(F32)

    n_da = DA_HEADS * (4 * DA_QK + DA_V)
    n_gla = 2 * GLA_HEADS * GLA_DK + 2 * GLA_HEADS * GLA_DV + GLA_RANK
    n_ssd = SSD_INNER + (SSD_INNER + 2 * SSD_GROUPS * SSD_N) + SSD_HEADS
    n_rw = 3 * RW_DIM + rw_w2.shape[1] + rw_a2.shape[1] + rw_g2.shape[1]
    o_gla, o_ssd, o_rw, o_gate = n_da, n_da + n_gla, n_da + n_gla + n_ssd, n_da + n_gla + n_ssd + n_rw
    pad128 = lambda n: -(-n // LANE) * LANE
    cast = lambda w: w.astype(MXU_DTYPE)

    for l in range(depth):
        h = _ffn(h, ffn1_norm[l], cast(ffn1_wg[l]), cast(ffn1_wu[l]), cast(ffn1_wd[l]), final_norm, final=False)

        wl = w_in[l]
        w_da = cast(wl[:, 0:n_da])
        w_gla = cast(_pad_cols(wl[:, o_gla:o_ssd], pad128(n_gla)))
        w_ssd = cast(_pad_cols(wl[:, o_ssd:o_rw], pad128(n_ssd)))
        w_rw = cast(wl[:, o_rw:o_gate])
        w_gate = cast(wl[:, o_gate:])
        g = mix_norm[l]

        lam_init = 0.8 - 0.6 * math.exp(-0.3 * l)
        lam = (jnp.exp(jnp.sum(da_lambda_q1[l] * da_lambda_k1[l])) - jnp.exp(jnp.sum(da_lambda_q2[l] * da_lambda_k2[l]))
               + lam_init).reshape(1, 1).astype(F32)
        pda = _proj_da(h, g, w_da, cos_t, sin_t, S)
        y_a = _diff_attention(lam, pda, da_norm[l], B, S, lam_init)

        pg = _proj(h, g, w_gla, F32)
        y_b = _gla(pg, _pad_rows_at(gla_gate_w2[l], 0, LANE), gla_gate_b[l], gla_norm[l], B, S)

        ps = _proj(h, g, w_ssd, F32)
        dtb_pad = _pad_cols(ssd_dt_bias[l].reshape(1, -1), LANE)
        a_pad = _pad_cols(-jnp.exp(ssd_a_log[l]).reshape(1, -1), LANE)
        d_x = jnp.repeat(ssd_d[l], SSD_P).reshape(1, -1)
        y_c = _ssd(ps, ssd_conv_w[l], ssd_conv_b[l], dtb_pad, a_pad, e64, e128, d_x, ssd_norm[l], B, S)

        pr = _proj(h, g, w_rw, F32)
        rank_w = rw_w2.shape[1]
        y_d = _rwkv(pr, rw_mu[l], rw_w0[l], _pad_rows_at(rw_w2[l], 0, LANE), rw_a0[l],
                    _pad_rows_at(rw_a2[l], rank_w, LANE), cast(rw_g2[l]), rw_k_k[l], rw_k_a[l], rw_r_k[l],
                    rw_norm_w[l], rw_norm_b[l], rw_bd, B, S)

        h = _merge(h, g, w_gate, gate_b[l], (y_a, y_b, y_c, y_d), cast(w_branch[l]), cast(w_out[l]))
        h = _ffn(h, ffn2_norm[l], cast(ffn2_wg[l]), cast(ffn2_wu[l]), cast(ffn2_wd[l]), final_norm,
                 final=(l == depth - 1))
    return h.reshape(B, S, D)
```

```python
import functools
import math

import jax
import jax.numpy as jnp
from jax import lax
from jax.experimental import pallas as pl
from jax.experimental.pallas import tpu as pltpu

F32 = jnp.float32
BF16 = jnp.bfloat16
MXU_DTYPE = BF16

CHUNK = 64
ROPE_THETA = 10000.0
EPS = 1e-6
DA_HEADS, DA_QK, DA_V = 4, 64, 128
GLA_HEADS, GLA_DK, GLA_DV, GLA_RANK, GLA_TAU = 4, 64, 128, 16, 16.0
SSD_HEADS, SSD_P, SSD_GROUPS, SSD_N, SSD_CONV = 8, 64, 2, 128, 4
SSD_INNER = SSD_HEADS * SSD_P
RW_HEADS, RW_HEAD = 8, 64
RW_DIM = RW_HEADS * RW_HEAD
RW_DECAY_SCALE = 0.606531
RW_GN_EPS = 64e-5
LANE = 128
SUBLANE = 8
VMEM_LIMIT = 48 * 1024 * 1024


def _cparams(sem):
    return pltpu.CompilerParams(dimension_semantics=sem, vmem_limit_bytes=VMEM_LIMIT)


def _mm(a, b):
    return jnp.dot(a.astype(MXU_DTYPE), b.astype(MXU_DTYPE), preferred_element_type=F32)


def _mm_nt(a, b):
    return lax.dot_general(a.astype(MXU_DTYPE), b.astype(MXU_DTYPE), (((1,), (1,)), ((), ())),
                           preferred_element_type=F32)


def _mm_tn(a, b):
    return lax.dot_general(a.astype(MXU_DTYPE), b.astype(MXU_DTYPE), (((0,), (0,)), ((), ())),
                           preferred_element_type=F32)


def _split(x):
    hi = x.astype(MXU_DTYPE)
    lo = (x - hi.astype(F32)).astype(MXU_DTYPE)
    return hi, lo


def _sel_l(m01, x):
    hi, lo = _split(x)
    m = m01.astype(MXU_DTYPE)
    return jnp.dot(m, hi, preferred_element_type=F32) + jnp.dot(m, lo, preferred_element_type=F32)


def _sel_r(x, m01):
    hi, lo = _split(x)
    m = m01.astype(MXU_DTYPE)
    return jnp.dot(hi, m, preferred_element_type=F32) + jnp.dot(lo, m, preferred_element_type=F32)


def _sigmoid(x):
    return 1.0 / (1.0 + jnp.exp(-x))


def _silu(x):
    return x * _sigmoid(x)


def _softplus(x):
    return jnp.maximum(x, 0.0) + jnp.log1p(jnp.exp(-jnp.abs(x)))


def _rmsnorm_rows(x, g):
    return x * lax.rsqrt(jnp.mean(x * x, axis=-1, keepdims=True) + EPS) * g


def _tri(n, strict=False):
    r = lax.broadcasted_iota(jnp.int32, (n, n), 0)
    c = lax.broadcasted_iota(jnp.int32, (n, n), 1)
    return (c < r) if strict else (c <= r)


def _ffn_kernel(h_ref, g_ref, wg_ref, wu_ref, wd_ref, fg_ref, o_ref, xn_s, act_s, *, tf, final):
    xn_s[...] = _rmsnorm_rows(h_ref[...], g_ref[...]).astype(xn_s.dtype)
    F = wg_ref.shape[1]
    for f in range(F // tf):
        cols = slice(f * tf, (f + 1) * tf)
        xn = xn_s[...]
        gate = jnp.dot(xn, wg_ref[:, cols], preferred_element_type=F32)
        up = jnp.dot(xn, wu_ref[:, cols], preferred_element_type=F32)
        act_s[:, cols] = (_silu(gate) * up).astype(act_s.dtype)
    out = h_ref[...] + 0.5 * jnp.dot(act_s[...], wd_ref[...], preferred_element_type=F32)
    if final:
        out = _rmsnorm_rows(out, fg_ref[...])
    o_ref[...] = out


def _resident(shape):
    return pl.BlockSpec(shape, lambda *_: (0,) * len(shape), pipeline_mode=pl.Buffered(1))


def _ffn(h, g, wg, wu, wd, fg, *, final, tm=256, tf=256):
    T, D = h.shape
    F = wg.shape[1]
    tm = min(tm, T)
    return pl.pallas_call(
        functools.partial(_ffn_kernel, tf=tf, final=final),
        grid=(T // tm,),
        in_specs=[
            pl.BlockSpec((tm, D), lambda i: (i, 0)),
            _resident((1, D)),
            _resident((D, F)),
            _resident((D, F)),
            _resident((F, D)),
            _resident((1, D)),
        ],
        out_specs=pl.BlockSpec((tm, D), lambda i: (i, 0)),
        out_shape=jax.ShapeDtypeStruct((T, D), F32),
        scratch_shapes=[pltpu.VMEM((tm, D), MXU_DTYPE), pltpu.VMEM((tm, F), MXU_DTYPE)],
        compiler_params=_cparams(("parallel",)),
        name="ffn",
    )(h, g.reshape(1, D), wg, wu, wd, fg.reshape(1, D))


def _proj_kernel(h_ref, g_ref, w_ref, o_ref):
    xn = _rmsnorm_rows(h_ref[...], g_ref[...]).astype(w_ref.dtype)
    o_ref[...] = jnp.dot(xn, w_ref[...], preferred_element_type=F32).astype(o_ref.dtype)


def _proj(h, g, w, out_dtype, *, tm=512):
    T, D = h.shape
    N = w.shape[1]
    tm = min(tm, T)
    return pl.pallas_call(
        _proj_kernel,
        grid=(T // tm,),
        in_specs=[
            pl.BlockSpec((tm, D), lambda i: (i, 0)),
            pl.BlockSpec((1, D), lambda i: (0, 0)),
            pl.BlockSpec((D, N), lambda i: (0, 0)),
        ],
        out_specs=pl.BlockSpec((tm, N), lambda i: (i, 0)),
        out_shape=jax.ShapeDtypeStruct((T, N), out_dtype),
        compiler_params=_cparams(("parallel",)),
        name="proj",
    )(h, g.reshape(1, D), w)


def _proj_da_kernel(h_ref, g_ref, w_ref, cos_ref, sin_ref, o_ref, *, nqk):
    xn = _rmsnorm_rows(h_ref[...], g_ref[...]).astype(w_ref.dtype)
    p = jnp.dot(xn, w_ref[...], preferred_element_type=F32)
    cos = cos_ref[...]
    sin = sin_ref[...]
    tm = p.shape[0]
    lane = lax.broadcasted_iota(jnp.int32, (tm, LANE), 1)
    first_half = (lane % DA_QK) < (DA_QK // 2)
    qscale = DA_QK ** -0.5 * math.log2(math.e)
    for blk in range(2 * nqk):
        t = p[:, blk * LANE:(blk + 1) * LANE]
        swapped = jnp.where(first_half, pltpu.roll(t, LANE - DA_QK // 2, 1), pltpu.roll(t, DA_QK // 2, 1))
        t = t * cos + swapped * sin
        if blk < nqk:
            t = t * qscale
        o_ref[:, blk * LANE:(blk + 1) * LANE] = t.astype(o_ref.dtype)
    ones = jnp.ones((tm, LANE), o_ref.dtype)
    for hd in range(DA_HEADS):
        src = (2 * nqk + hd) * LANE
        dst = (2 * nqk + 2 * hd) * LANE
        o_ref[:, dst:dst + LANE] = p[:, src:src + LANE].astype(o_ref.dtype)
        o_ref[:, dst + LANE:dst + 2 * LANE] = ones


def _proj_da(h, g, w, cos_t, sin_t, S, *, tm=512):
    T, D = h.shape
    N = w.shape[1] + DA_HEADS * LANE
    tm = min(tm, S)
    ns = S // tm
    nqk = DA_HEADS * 2 * DA_QK // LANE
    return pl.pallas_call(
        functools.partial(_proj_da_kernel, nqk=nqk),
        grid=(T // tm,),
        in_specs=[
            pl.BlockSpec((tm, D), lambda i: (i, 0)),
            pl.BlockSpec((1, D), lambda i: (0, 0)),
            pl.BlockSpec(w.shape, lambda i: (0, 0)),
            pl.BlockSpec((tm, LANE), lambda i: (i % ns, 0)),
            pl.BlockSpec((tm, LANE), lambda i: (i % ns, 0)),
        ],
        out_specs=pl.BlockSpec((tm, N), lambda i: (i, 0)),
        out_shape=jax.ShapeDtypeStruct((T, N), MXU_DTYPE),
        compiler_params=_cparams(("parallel",)),
        name="proj_da",
    )(h, g.reshape(1, D), w, cos_t, sin_t)


def _da_kernel(lam_ref, q_ref, k_ref, v_ref, g_ref, o_ref, q_s, m_s, l_s, acc_s, *, tq, nsub, out_scale):
    qi = pl.program_id(2)
    q = q_ref[...]
    lane = lax.broadcasted_iota(jnp.int32, q.shape, 1)
    zero = jnp.zeros_like(q)
    q_s[0] = jnp.where(lane < DA_QK, q, zero)
    q_s[1] = jnp.where(lane >= DA_QK, q, zero)
    m_s[...] = jnp.full_like(m_s, -jnp.inf)
    l_s[...] = jnp.zeros_like(l_s)
    acc_s[...] = jnp.zeros_like(acc_s)
    tr = tq // nsub
    reps = tq // LANE
    chains = [(c, pl.ds(r * tr, tr), r) for r in range(nsub) for c in range(2)]

    def step(j, masked):
        rows = pl.ds(pl.multiple_of(j * tq, tq), tq)
        k = k_ref[rows, :]
        v = v_ref[rows, :]
        s = [lax.dot_general(q_s[c, rs, :], k, (((1,), (1,)), ((), ())), preferred_element_type=F32)
             for c, rs, _ in chains]
        if masked:
            col = lax.broadcasted_iota(jnp.int32, (tr, tq), 1) // CHUNK
            row = lax.broadcasted_iota(jnp.int32, (tr, tq), 0)
            s = [jnp.where(col <= (row + r * tr) // CHUNK, x, -jnp.inf) for x, (_, _, r) in zip(s, chains)]
        m_prev = [m_s[c, rs, :] for c, rs, _ in chains]
        m_new = [jnp.maximum(mp, jnp.max(x, axis=-1, keepdims=True)) for mp, x in zip(m_prev, s)]
        p = [jnp.exp2(x - jnp.concatenate([mn] * reps, axis=1)) for x, mn in zip(s, m_new)]
        alpha = [jnp.exp2(mp - mn) for mp, mn in zip(m_prev, m_new)]
        pv = [jnp.dot(x.astype(v.dtype), v, preferred_element_type=F32) for x in p]
        for (c, rs, _), mn, a, o in zip(chains, m_new, alpha, pv):
            m_s[c, rs, :] = mn
            l_s[c, rs, :] = a * l_s[c, rs, :] + o[:, DA_V:]
            acc_s[c, rs, :] = a * acc_s[c, rs, :] + o[:, :DA_V]

    def block_pair(i, carry):
        step(2 * i, False)
        step(2 * i + 1, False)
        return carry

    lax.fori_loop(0, qi // 2, block_pair, 0)

    @pl.when(qi % 2 == 1)
    def _():
        step(qi - 1, False)

    step(qi, True)
    lam = lam_ref[0, 0]
    o = acc_s[0] / l_s[0] - lam * (acc_s[1] / l_s[1])
    o = _rmsnorm_rows(o, g_ref[...]) * out_scale
    o_ref[...] = o.astype(o_ref.dtype)


def _diff_attention(lam, pda, da_norm, B, S, lam_init, *, tq=512, nsub=2):
    T = pda.shape[0]
    tq = min(tq, S)
    nq = S // tq
    H = DA_HEADS
    kernel = functools.partial(_da_kernel, tq=tq, nsub=nsub, out_scale=1.0 - lam_init)
    return pl.pallas_call(
        kernel,
        grid=(B, H, nq),
        in_specs=[
            pl.BlockSpec(memory_space=pltpu.SMEM),
            pl.BlockSpec((tq, LANE), lambda b, h, i: (b * nq + i, h)),
            pl.BlockSpec((S, LANE), lambda b, h, i: (b, H + h)),
            pl.BlockSpec((S, 2 * LANE), lambda b, h, i: (b, H + h)),
            pl.BlockSpec((1, LANE), lambda b, h, i: (0, 0)),
        ],
        out_specs=pl.BlockSpec((tq, LANE), lambda b, h, i: (b * nq + i, h)),
        out_shape=jax.ShapeDtypeStruct((T, H * DA_V), MXU_DTYPE),
        scratch_shapes=[
            pltpu.VMEM((2, tq, LANE), MXU_DTYPE),
            pltpu.VMEM((2, tq, LANE), F32),
            pltpu.VMEM((2, tq, LANE), F32),
            pltpu.VMEM((2, tq, DA_V), F32),
        ],
        compiler_params=_cparams(("parallel", "parallel", "arbitrary")),
        name="diff_attn",
    )(lam, pda, pda, pda, da_norm.reshape(1, DA_V))


def _gla_kernel(p_ref, w2_ref, gb_ref, ng_ref, o_ref, st_s, lg_s, *, tc, nb, ahead):
    i = pl.program_id(0)
    HK = GLA_HEADS * GLA_DK
    HV = GLA_HEADS * GLA_DV

    @pl.when(i == 0)
    def _():
        st_s[...] = jnp.zeros_like(st_s)

    for b in range(nb):
        z = _mm(p_ref[b, :, 2 * HK + 2 * HV:], w2_ref[...]) + gb_ref[...]
        lg_s[b] = (jnp.minimum(z, 0.0) - jnp.log1p(jnp.exp(-jnp.abs(z)))) * (1.0 / GLA_TAU)

    tri = _tri(CHUNK).astype(F32)
    causal = _tri(CHUNK)
    lane_k = lax.broadcasted_iota(jnp.int32, (CHUNK, HK), 1) // GLA_DK
    bd = (lax.broadcasted_iota(jnp.int32, (HV, HK), 0) // GLA_DV
          == lax.broadcasted_iota(jnp.int32, (HV, HK), 1) // GLA_DK)
    ng = ng_ref[...]

    def body(c, carry):
        inst = [(b, pl.ds(pl.multiple_of((c * ahead + u) * CHUNK, CHUNK), CHUNK))
                for u in range(ahead) for b in range(nb)]
        q = [p_ref[b, rows, 0:HK] * (GLA_DK ** -0.5) for b, rows in inst]
        k = [p_ref[b, rows, HK:2 * HK] for b, rows in inst]
        v = [p_ref[b, rows, 2 * HK:2 * HK + HV].astype(MXU_DTYPE) for b, rows in inst]
        G = [_sel_l(tri, lg_s[b, rows, :]) for b, rows in inst]
        g_mid = [x[CHUNK // 2:CHUNK // 2 + 1] for x in G]
        g_last = [x[CHUNK - 1:CHUNK] for x in G]
        qe = [x * jnp.exp(g - gm) for x, g, gm in zip(q, G, g_mid)]
        ke = [(x * jnp.exp(gm - g)).astype(MXU_DTYPE) for x, g, gm in zip(k, G, g_mid)]
        q_in = [x * jnp.exp(g) for x, g in zip(q, G)]
        k_end = [x * jnp.exp(gl - g) for x, g, gl in zip(k, G, g_last)]
        att = [[jnp.where(causal, _mm_nt(jnp.where(lane_k == h, x, 0.0), y), 0.0) for x, y in zip(qe, ke)]
               for h in range(GLA_HEADS)]
        y_intra = [[_mm(a, x[:, h * GLA_DV:(h + 1) * GLA_DV]) for a, x in zip(att[h], v)]
                   for h in range(GLA_HEADS)]
        kv = [jnp.where(bd, _mm_tn(x, y), 0.0) for x, y in zip(v, k_end)]
        for n, (b, rows) in enumerate(inst):
            st = st_s[b]
            y_inter = _mm_nt(q_in[n], st)
            st_s[b] = st * jnp.exp(g_last[n]) + kv[n]
            outs = [_rmsnorm_rows(y_intra[h][n] + y_inter[:, h * GLA_DV:(h + 1) * GLA_DV], ng)
                    for h in range(GLA_HEADS)]
            og = p_ref[b, rows, 2 * HK + HV:2 * HK + 2 * HV]
            o_ref[b, rows, :] = (jnp.concatenate(outs, axis=1) * _silu(og)).astype(o_ref.dtype)
        return carry

    lax.fori_loop(0, tc // (CHUNK * ahead), body, 0)


def _gla(pg, w2pad, gate_b, gla_norm, B, S, *, tc=512, ahead=2):
    T, N = pg.shape
    tc = min(tc, S)
    ns = S // tc
    HK, HV = GLA_HEADS * GLA_DK, GLA_HEADS * GLA_DV
    const = lambda i: (0, 0)
    out = pl.pallas_call(
        functools.partial(_gla_kernel, tc=tc, nb=B, ahead=ahead),
        grid=(ns,),
        in_specs=[
            pl.BlockSpec((B, tc, N), lambda i: (0, i, 0)),
            pl.BlockSpec((LANE, HK), const),
            pl.BlockSpec((1, HK), const),
            pl.BlockSpec((1, GLA_DV), const),
        ],
        out_specs=pl.BlockSpec((B, tc, HV), lambda i: (0, i, 0)),
        out_shape=jax.ShapeDtypeStruct((B, S, HV), MXU_DTYPE),
        scratch_shapes=[pltpu.VMEM((B, HV, HK), F32), pltpu.VMEM((B, tc, HK), F32)],
        compiler_params=_cparams(("arbitrary",)),
        name="gla",
    )(pg.reshape(B, S, N), w2pad, gate_b.reshape(1, HK), gla_norm.reshape(1, GLA_DV))
    return out.reshape(T, HV)


def _ssd_kernel(p_ref, cw_ref, cb_ref, dtb_ref, a_ref, e64_ref, e128_ref, dx_ref, ng_ref, o_ref,
                st_s, xpad_s, xs_s, bc_s, ax_s, ax2_s, xdt_s, *, tc, nb, ahead):
    i = pl.program_id(0)
    NI = SSD_INNER
    GN = SSD_GROUPS * SSD_N
    CD = NI + 2 * GN
    PAD = SUBLANE

    @pl.when(i == 0)
    def _():
        st_s[...] = jnp.zeros_like(st_s)
        xpad_s[:, 0:PAD, :] = jnp.zeros((nb, PAD, CD), F32)

    for b in range(nb):
        xpad_s[b, PAD:PAD + tc, :] = p_ref[b, :, NI:NI + CD]
        conv = cb_ref[...]
        for kk in range(SSD_CONV):
            off = PAD - (SSD_CONV - 1) + kk
            conv = conv + cw_ref[kk:kk + 1, :] * xpad_s[b, off:off + tc, :]
        xpad_s[b, 0:PAD, :] = xpad_s[b, tc:tc + PAD, :]
        xbc = _silu(conv)
        xs_s[b] = xbc[:, 0:NI]
        bc_s[b] = xbc[:, NI:]
        dt = _softplus(p_ref[b, :, NI + CD:] + dtb_ref[...])
        dta = dt * a_ref[...]
        ax_s[b] = _sel_r(dta, e64_ref[...])
        ax2_s[b] = _sel_r(dta, e128_ref[...])
        xdt_s[b] = xbc[:, 0:NI] * _sel_r(dt, e64_ref[...])

    tri = _tri(CHUNK).astype(F32)
    tri_t = (lax.broadcasted_iota(jnp.int32, (CHUNK, CHUNK), 0)
             <= lax.broadcasted_iota(jnp.int32, (CHUNK, CHUNK), 1)).astype(F32)
    causal = _tri(CHUNK)
    low_half = lax.broadcasted_iota(jnp.int32, (CHUNK, LANE), 1) < SSD_P
    rep = SSD_HEADS // SSD_GROUPS
    GW = rep * SSD_P

    def body(c, carry):
        inst = [(b, pl.ds(pl.multiple_of((c * ahead + u) * CHUNK, CHUNK), CHUNK))
                for u in range(ahead) for b in range(nb)]
        xdt = [xdt_s[b, rows, :] for b, rows in inst]
        acx = [_sel_l(tri, ax_s[b, rows, :]) for b, rows in inst]
        acx2 = [_sel_l(tri, ax2_s[b, rows, :]) for b, rows in inst]
        dta2 = [ax2_s[b, rows, :] for b, rows in inst]
        a_last = [x[CHUNK - 1:CHUNK] for x in acx]
        x_end = [(x * jnp.exp(al - a)).astype(MXU_DTYPE) for x, a, al in zip(xdt, acx, a_last)]
        xdt_m = [x.astype(MXU_DTYPE) for x in xdt]
        bm = [[bc_s[b, rows, g * SSD_N:(g + 1) * SSD_N].astype(MXU_DTYPE) for b, rows in inst]
              for g in range(SSD_GROUPS)]
        cm = [[bc_s[b, rows, GN + g * SSD_N:GN + (g + 1) * SSD_N].astype(MXU_DTYPE) for b, rows in inst]
              for g in range(SSD_GROUPS)]
        cb = [[_mm_nt(x, y) for x, y in zip(cm[g], bm[g])] for g in range(SSD_GROUPS)]
        new_st = [jnp.concatenate([_mm_tn(bm[g][n], x_end[n][:, g * GW:(g + 1) * GW])
                                   for g in range(SSD_GROUPS)], axis=1) for n in range(len(inst))]
        y_intra = []
        for n in range(len(inst)):
            parts = []
            for pair in range(SSD_HEADS // 2):
                res = []
                for hh in range(2):
                    h = pair * 2 + hh
                    col = acx2[n][:, h * LANE:h * LANE + CHUNK]
                    row = jnp.sum(tri_t * dta2[n][:, h * LANE:h * LANE + CHUNK], axis=0, keepdims=True)
                    lmat = jnp.exp(jnp.where(causal, col - row, -jnp.inf))
                    res.append(_mm(cb[h // rep][n] * lmat, xdt_m[n][:, pair * LANE:(pair + 1) * LANE]))
                parts.append(jnp.where(low_half, res[0], res[1]))
            y_intra.append(jnp.concatenate(parts, axis=1))
        half = NI // SSD_GROUPS
        for n, (b, rows) in enumerate(inst):
            st = st_s[b]
            inter = jnp.concatenate([_mm(cm[g][n], st[:, g * GW:(g + 1) * GW]) for g in range(SSD_GROUPS)],
                                    axis=1)
            st_s[b] = st * jnp.exp(a_last[n]) + new_st[n]
            y = y_intra[n] + inter * jnp.exp(acx[n]) + dx_ref[...] * xs_s[b, rows, :]
            y = y * _silu(p_ref[b, rows, 0:NI])
            outs = [_rmsnorm_rows(y[:, g * half:(g + 1) * half], ng_ref[:, g * half:(g + 1) * half])
                    for g in range(SSD_GROUPS)]
            o_ref[b, rows, :] = jnp.concatenate(outs, axis=1).astype(o_ref.dtype)
        return carry

    lax.fori_loop(0, tc // (CHUNK * ahead), body, 0)


def _ssd(ps, conv_w, conv_b, dtb_pad, a_pad, e64, e128, d_x, ssd_norm, B, S, *, tc=512, ahead=2):
    T, N = ps.shape
    tc = min(tc, S)
    ns = S // tc
    NI = SSD_INNER
    CD = NI + 2 * SSD_GROUPS * SSD_N
    const = lambda i: (0, 0)
    out = pl.pallas_call(
        functools.partial(_ssd_kernel, tc=tc, nb=B, ahead=ahead),
        grid=(ns,),
        in_specs=[
            pl.BlockSpec((B, tc, N), lambda i: (0, i, 0)),
            pl.BlockSpec((SSD_CONV, CD), const),
            pl.BlockSpec((1, CD), const),
            pl.BlockSpec((1, LANE), const),
            pl.BlockSpec((1, LANE), const),
            pl.BlockSpec((LANE, NI), const),
            pl.BlockSpec((LANE, SSD_HEADS * LANE), const),
            pl.BlockSpec((1, NI), const),
            pl.BlockSpec((1, NI), const),
        ],
        out_specs=pl.BlockSpec((B, tc, NI), lambda i: (0, i, 0)),
        out_shape=jax.ShapeDtypeStruct((B, S, NI), MXU_DTYPE),
        scratch_shapes=[
            pltpu.VMEM((B, SSD_N, NI), F32),
            pltpu.VMEM((B, tc + SUBLANE, CD), F32),
            pltpu.VMEM((B, tc, NI), F32),
            pltpu.VMEM((B, tc, 2 * SSD_GROUPS * SSD_N), F32),
            pltpu.VMEM((B, tc, NI), F32),
            pltpu.VMEM((B, tc, SSD_HEADS * LANE), F32),
            pltpu.VMEM((B, tc, NI), F32),
        ],
        compiler_params=_cparams(("arbitrary",)),
        name="ssd",
    )(ps.reshape(B, S, N), conv_w, conv_b.reshape(1, CD), dtb_pad, a_pad, e64, e128, d_x,
      ssd_norm.reshape(1, NI))
    return out.reshape(T, NI)


def _unit_lower_inverse(mats, eye, same_blk, stack):
    d = [jnp.where(same_blk, a, 0.0) for a in mats]
    o = [a - x for a, x in zip(mats, d)]
    t = [eye + x for x in d]
    dp = d
    sd = [stack(x) for x in dp]
    for _ in range(3):
        dp = [_mm(x, y) for x, y in zip(dp, sd)]
        sd = [stack(x) for x in dp]
        t = [x + _mm(x, y) for x, y in zip(t, sd)]
    n = [_mm(x, stack(y)) for x, y in zip(t, o)]
    n2 = [_mm(x, stack(x)) for x in n]
    q = [eye + x for x in n]
    q = [x + _mm(x, stack(y)) for x, y in zip(q, n2)]
    return [_mm(x, stack(y)) for x, y in zip(q, t)]


def _rw_kernel(p_ref, mu_ref, w0_ref, w2_ref, a0_ref, a2_ref, g2_ref, kk_ref, ka_ref, rk_ref,
               nw_ref, nb_ref, bd_ref, o_ref,
               mt_s, ppad_s, r_s, k_s, v_s, lw_s, al_s, be_s, g_s, *, tc, nb, ahead):
    i = pl.program_id(0)
    R = RW_DIM
    PAD = SUBLANE
    NP = p_ref.shape[2]
    NPAIR = RW_HEADS // 2

    @pl.when(i == 0)
    def _():
        mt_s[...] = jnp.zeros_like(mt_s)
        ppad_s[:, 0:PAD, :] = jnp.zeros((nb, PAD, NP), F32)

    bd = bd_ref[...]
    for b in range(nb):
        p = p_ref[b]
        ppad_s[b, PAD:PAD + tc, :] = p
        prev = ppad_s[b, PAD - 1:PAD - 1 + tc, :]
        ppad_s[b, 0:PAD, :] = ppad_s[b, tc:tc + PAD, :]
        pm = p + (prev - p) * mu_ref[...]
        r = pm[:, 0:R]
        k = pm[:, R:2 * R]
        v = pm[:, 2 * R:3 * R]
        wa = pm[:, 3 * R:3 * R + LANE]
        gl = pm[:, 3 * R + LANE:3 * R + 2 * LANE]
        lw_s[b] = -RW_DECAY_SCALE * _sigmoid(w0_ref[...] + _mm(jnp.tanh(wa), w2_ref[...]))
        a = _sigmoid(a0_ref[...] + _mm(wa, a2_ref[...]))
        g_s[b] = _mm(_sigmoid(gl), g2_ref[...])
        kk = k * kk_ref[...]
        kk = kk / jnp.maximum(jnp.sqrt(_sel_r(kk * kk, bd)), 1e-12)
        r_s[b] = r
        k_s[b] = k * (1.0 + (a - 1.0) * ka_ref[...])
        v_s[b] = v
        al_s[b] = -kk
        be_s[b] = kk * a

    C2 = 2 * CHUNK
    tri = _tri(CHUNK).astype(F32)
    tt = lax.broadcasted_iota(jnp.int32, (CHUNK, C2), 0)
    ss = lax.broadcasted_iota(jnp.int32, (CHUNK, C2), 1) % CHUNK
    strict = ss < tt
    incl = ss <= tt
    eye = (ss == tt).astype(F32)
    same_blk = (tt // 16) == (ss // 16)
    m0 = lax.broadcasted_iota(jnp.int32, (CHUNK, LANE), 1) < RW_HEAD
    bdm = (lax.broadcasted_iota(jnp.int32, (C2, C2), 0) // CHUNK
           == lax.broadcasted_iota(jnp.int32, (C2, C2), 1) // CHUNK)

    def stack(x):
        xb = x.astype(MXU_DTYPE)
        zero = jnp.zeros_like(xb)
        return jnp.concatenate([jnp.where(m0, xb, zero), jnp.where(m0, zero, xb)], axis=0)

    def body(c, carry):
        row_sl = [pl.ds(pl.multiple_of((c * ahead + u) * CHUNK, CHUNK), CHUNK) for u in range(ahead)]
        w_end, vv_b, wide = [], [], []
        for rows in row_sl:
            for b in range(nb):
                lw = lw_s[b, rows, :]
                cum = _sel_l(tri, lw)
                e_cum = jnp.exp(cum)
                e_inv = jnp.exp(-cum)
                ah = al_s[b, rows, :] * jnp.exp(cum - lw)
                bh = be_s[b, rows, :] * e_inv
                kh = k_s[b, rows, :] * e_inv
                rh = r_s[b, rows, :] * e_cum
                vv = v_s[b, rows, :]
                vv_b.append(vv)
                for j in range(NPAIR):
                    sl = slice(j * LANE, (j + 1) * LANE)
                    w_end.append(e_cum[CHUNK - 1:CHUNK, sl])
                    wide.append(tuple(x[:, sl].astype(MXU_DTYPE) for x in (ah, rh, bh, kh, vv)))
        nch = nb * NPAIR
        ar = [jnp.concatenate([w[0], w[1]], axis=0) for w in wide]
        bk = [jnp.concatenate([stack(w[2]), stack(w[3])], axis=0) for w in wide]
        v_bd = [stack(w[4]) for w in wide]
        big = [_mm_nt(x, y) for x, y in zip(ar, bk)]
        a_ab = [jnp.where(strict, x[0:CHUNK, 0:C2], 0.0) for x in big]
        a_ak = [jnp.where(strict, x[0:CHUNK, C2:], 0.0) for x in big]
        r_b = [jnp.where(incl, x[CHUNK:, 0:C2], 0.0) for x in big]
        r_k = [jnp.where(incl, x[CHUNK:, C2:], 0.0) for x in big]
        t_inv = _unit_lower_inverse(a_ab, eye, same_blk, stack)
        for u, rows in enumerate(row_sl):
            ch = slice(u * nch, (u + 1) * nch)
            mt = [mt_s[n] for n in range(nch)]
            rhs = [_mm_nt(w[0], m) + _mm(x, v) for w, m, x, v in zip(wide[ch], mt, a_ak[ch], v_bd[ch])]
            u_w = [_mm(x, stack(y)) for x, y in zip(t_inv[ch], rhs)]
            upd = [_mm_tn(jnp.concatenate([x.astype(MXU_DTYPE), w[4]], axis=0),
                          jnp.concatenate([w[2], w[3]], axis=0)) for x, w in zip(u_w, wide[ch])]
            for n in range(nch):
                mt_s[n] = (mt[n] + jnp.where(bdm, upd[n], 0.0)) * w_end[u * nch + n]
            y_w = [_mm_nt(w[1], m) + _mm(x, stack(uu)) + _mm(z, v)
                   for w, m, x, uu, z, v in zip(wide[ch], mt, r_b[ch], u_w, r_k[ch], v_bd[ch])]
            for b in range(nb):
                y = jnp.concatenate([y_w[b * NPAIR + j] for j in range(NPAIR)], axis=1)
                mean = _sel_r(y, bd) * (1.0 / RW_HEAD)
                yc = y - mean
                var = _sel_r(yc * yc, bd) * (1.0 / RW_HEAD)
                yn = yc * lax.rsqrt(var + RW_GN_EPS) * nw_ref[...] + nb_ref[...]
                bonus = _sel_r(r_s[b, rows, :] * k_s[b, rows, :] * rk_ref[...], bd) * vv_b[u * nb + b]
                o_ref[b, rows, :] = ((yn + bonus) * g_s[b, rows, :]).astype(o_ref.dtype)
        return carry

    lax.fori_loop(0, tc // (CHUNK * ahead), body, 0)


def _rwkv(pr, mu, w0, w2pad, a0, a2pad, g2, k_k, k_a, r_k, norm_w, norm_b, bd, B, S, *, tc=256, ahead=4):
    T, N = pr.shape
    tc = min(tc, S)
    ns = S // tc
    R = RW_DIM
    const = lambda i: (0, 0)
    row = lambda x: x.reshape(1, -1)
    vec = pl.BlockSpec((1, R), const)
    out = pl.pallas_call(
        functools.partial(_rw_kernel, tc=tc, nb=B, ahead=ahead),
        grid=(ns,),
        in_specs=[
            pl.BlockSpec((B, tc, N), lambda i: (0, i, 0)),
            pl.BlockSpec((1, N), const),
            vec,
            pl.BlockSpec((LANE, R), const),
            vec,
            pl.BlockSpec((LANE, R), const),
            pl.BlockSpec((LANE, R), const),
            vec, vec, vec, vec, vec,
            pl.BlockSpec((R, R), const),
        ],
        out_specs=pl.BlockSpec((B, tc, R), lambda i: (0, i, 0)),
        out_shape=jax.ShapeDtypeStruct((B, S, R), MXU_DTYPE),
        scratch_shapes=[
            pltpu.VMEM((B * RW_HEADS // 2, LANE, LANE), F32),
            pltpu.VMEM((B, tc + SUBLANE, N), F32),
        ] + [pltpu.VMEM((B, tc, R), F32) for _ in range(7)],
        compiler_params=_cparams(("arbitrary",)),
        name="rwkv7",
    )(pr.reshape(B, S, N), row(mu), row(w0), w2pad, row(a0), a2pad, g2, row(k_k), row(k_a), row(r_k),
      row(norm_w), row(norm_b), bd)
    return out.reshape(T, R)


def _merge_kernel(h_ref, g_ref, wgate_ref, gb_ref, ya_ref, yb_ref, yc_ref, yd_ref, wb_ref, wo_ref, o_ref):
    h = h_ref[...]
    D = h.shape[1]
    xn = _rmsnorm_rows(h, g_ref[...]).astype(wgate_ref.dtype)
    merged = jnp.zeros(h.shape, F32)
    for n, y_ref in enumerate((ya_ref, yb_ref, yc_ref, yd_ref)):
        logits = jnp.dot(xn, wgate_ref[:, n * D:(n + 1) * D], preferred_element_type=F32) + gb_ref[n:n + 1, :]
        proj = jnp.dot(y_ref[...], wb_ref[n], preferred_element_type=F32)
        merged = merged + _sigmoid(logits) * proj
    o_ref[...] = h + jnp.dot(merged.astype(wo_ref.dtype), wo_ref[...], preferred_element_type=F32)


def _merge(h, g, wgate, gate_b, ys, w_branch, w_out, *, tm=256):
    T, D = h.shape
    tm = min(tm, T)
    NB, MW, _ = w_branch.shape
    const2 = lambda i: (0, 0)
    ytile = pl.BlockSpec((tm, MW), lambda i: (i, 0))
    return pl.pallas_call(
        _merge_kernel,
        grid=(T // tm,),
        in_specs=[
            pl.BlockSpec((tm, D), lambda i: (i, 0)),
            pl.BlockSpec((1, D), const2),
            pl.BlockSpec((D, NB * D), const2),
            pl.BlockSpec((NB, D), const2),
            ytile, ytile, ytile, ytile,
            pl.BlockSpec((NB, MW, D), lambda i: (0, 0, 0)),
            pl.BlockSpec((D, D), const2),
        ],
        out_specs=pl.BlockSpec((tm, D), lambda i: (i, 0)),
        out_shape=jax.ShapeDtypeStruct((T, D), F32),
        compiler_params=_cparams(("parallel",)),
        name="merge",
    )(h, g.reshape(1, D), wgate, gate_b, *ys, w_branch, w_out)


def _pad_cols(w, n):
    return jnp.pad(w, ((0, 0), (0, n - w.shape[1])))


def _pad_rows_at(w, start, total):
    return jnp.pad(w, ((start, total - start - w.shape[0]), (0, 0)))


def _rope_tables(S):
    half = DA_QK // 2
    inv_freq = ROPE_THETA ** (-jnp.arange(half, dtype=F32) / half)
    ang = jnp.arange(S, dtype=F32)[:, None] * inv_freq[None, :]
    cos, sin = jnp.cos(ang), jnp.sin(ang)
    reps = LANE // DA_QK
    return jnp.tile(jnp.concatenate([cos, cos], axis=1), (1, reps)), jnp.tile(jnp.concatenate([-sin, sin], axis=1), (1, reps))


def _head_expand(heads, width):
    r = jnp.arange(LANE)[:, None]
    c = jnp.arange(heads * width)[None, :] // width
    return (r == c).astype(F32)


def kernel(x, ffn1_norm, ffn1_wg, ffn1_wu, ffn1_wd, mix_norm, w_in, da_lambda_q1, da_lambda_k1, da_lambda_q2, da_lambda_k2, da_norm, gla_gate_w2, gla_gate_b, gla_norm, ssd_conv_w, ssd_conv_b, ssd_dt_bias, ssd_a_log, ssd_d, ssd_norm, rw_mu, rw_w0, rw_w2, rw_a0, rw_a2, rw_g2, rw_k_k, rw_k_a, rw_r_k, rw_norm_w, rw_norm_b, w_branch, gate_b, w_out, ffn2_norm, ffn2_wg, ffn2_wu, ffn2_wd, final_norm):
    B, S, D = x.shape
    depth = w_in.shape[0]
    T = B * S
    h = x.reshape(T, D)
    cos_t, sin_t = _rope_tables(S)
    e64 = _head_expand(SSD_HEADS, SSD_P)
    e128 = _head_expand(SSD_HEADS, LANE)
    rw_bd = (jnp.arange(RW_DIM)[:, None] // RW_HEAD == jnp.arange(RW_DIM)[None, :] // RW_HEAD).astype(F32)

    n_da = DA_HEADS * (4 * DA_QK + DA_V)
    n_gla = 2 * GLA_HEADS * GLA_DK + 2 * GLA_HEADS * GLA_DV + GLA_RANK
    n_ssd = SSD_INNER + (SSD_INNER + 2 * SSD_GROUPS * SSD_N) + SSD_HEADS
    n_rw = 3 * RW_DIM + rw_w2.shape[1] + rw_a2.shape[1] + rw_g2.shape[1]
    o_gla, o_ssd, o_rw, o_gate = n_da, n_da + n_gla, n_da + n_gla + n_ssd, n_da + n_gla + n_ssd + n_rw
    pad128 = lambda n: -(-n // LANE) * LANE
    cast = lambda w: w.astype(MXU_DTYPE)

    for l in range(depth):
        h = _ffn(h, ffn1_norm[l], cast(ffn1_wg[l]), cast(ffn1_wu[l]), cast(ffn1_wd[l]), final_norm, final=False)

        wl = w_in[l]
        w_da = cast(wl[:, 0:n_da])
        w_gla = cast(_pad_cols(wl[:, o_gla:o_ssd], pad128(n_gla)))
        w_ssd = cast(_pad_cols(wl[:, o_ssd:o_rw], pad128(n_ssd)))
        w_rw = cast(wl[:, o_rw:o_gate])
        w_gate = cast(wl[:, o_gate:])
        g = mix_norm[l]

        lam_init = 0.8 - 0.6 * math.exp(-0.3 * l)
        lam = (jnp.exp(jnp.sum(da_lambda_q1[l] * da_lambda_k1[l])) - jnp.exp(jnp.sum(da_lambda_q2[l] * da_lambda_k2[l]))
               + lam_init).reshape(1, 1).astype(F32)
        pda = _proj_da(h, g, w_da, cos_t, sin_t, S)
        y_a = _diff_attention(lam, pda, da_norm[l], B, S, lam_init)

        pg = _proj(h, g, w_gla, F32)
        y_b = _gla(pg, _pad_rows_at(gla_gate_w2[l], 0, LANE), gla_gate_b[l], gla_norm[l], B, S)

        ps = _proj(h, g, w_ssd, F32)
        dtb_pad = _pad_cols(ssd_dt_bias[l].reshape(1, -1), LANE)
        a_pad = _pad_cols(-jnp.exp(ssd_a_log[l]).reshape(1, -1), LANE)
        d_x = jnp.repeat(ssd_d[l], SSD_P).reshape(1, -1)
        y_c = _ssd(ps, ssd_conv_w[l], ssd_conv_b[l], dtb_pad, a_pad, e64, e128, d_x, ssd_norm[l], B, S)

        pr = _proj(h, g, w_rw, F32)
        rank_w = rw_w2.shape[1]
        y_d = _rwkv(pr, rw_mu[l], rw_w0[l], _pad_rows_at(rw_w2[l], 0, LANE), rw_a0[l],
                    _pad_rows_at(rw_a2[l], rank_w, LANE), cast(rw_g2[l]), rw_k_k[l], rw_k_a[l], rw_r_k[l],
                    rw_norm_w[l], rw_norm_b[l], rw_bd, B, S)

        h = _merge(h, g, w_gate, gate_b[l], (y_a, y_b, y_c, y_d), cast(w_branch[l]), cast(w_out[l]))
        h = _ffn(h, ffn2_norm[l], cast(ffn2_wg[l]), cast(ffn2_wu[l]), cast(ffn2_wd[l]), final_norm,
                 final=(l == depth - 1))
    return h.reshape(B, S, D)
```

```python
import functools
import math

import jax
import jax.numpy as jnp
from jax import lax
from jax.experimental import pallas as pl
from jax.experimental.pallas import tpu as pltpu

F32 = jnp.float32
BF16 = jnp.bfloat16
MXU_DTYPE = BF16

CHUNK = 64
ROPE_THETA = 10000.0
EPS = 1e-6
DA_HEADS, DA_QK, DA_V = 4, 64, 128
GLA_HEADS, GLA_DK, GLA_DV, GLA_RANK, GLA_TAU = 4, 64, 128, 16, 16.0
SSD_HEADS, SSD_P, SSD_GROUPS, SSD_N, SSD_CONV = 8, 64, 2, 128, 4
SSD_INNER = SSD_HEADS * SSD_P
RW_HEADS, RW_HEAD = 8, 64
RW_DIM = RW_HEADS * RW_HEAD
RW_DECAY_SCALE = 0.606531
RW_GN_EPS = 64e-5
LANE = 128
SUBLANE = 8
VMEM_LIMIT = 48 * 1024 * 1024


def _cparams(sem):
    return pltpu.CompilerParams(dimension_semantics=sem, vmem_limit_bytes=VMEM_LIMIT)


def _mm(a, b):
    return jnp.dot(a.astype(MXU_DTYPE), b.astype(MXU_DTYPE), preferred_element_type=F32)


def _mm_nt(a, b):
    return lax.dot_general(a.astype(MXU_DTYPE), b.astype(MXU_DTYPE), (((1,), (1,)), ((), ())),
                           preferred_element_type=F32)


def _mm_tn(a, b):
    return lax.dot_general(a.astype(MXU_DTYPE), b.astype(MXU_DTYPE), (((0,), (0,)), ((), ())),
                           preferred_element_type=F32)


def _split(x):
    hi = x.astype(MXU_DTYPE)
    lo = (x - hi.astype(F32)).astype(MXU_DTYPE)
    return hi, lo


def _sel_l(m01, x):
    hi, lo = _split(x)
    m = m01.astype(MXU_DTYPE)
    return jnp.dot(m, hi, preferred_element_type=F32) + jnp.dot(m, lo, preferred_element_type=F32)


def _sel_r(x, m01):
    hi, lo = _split(x)
    m = m01.astype(MXU_DTYPE)
    return jnp.dot(hi, m, preferred_element_type=F32) + jnp.dot(lo, m, preferred_element_type=F32)


def _sigmoid(x):
    return 1.0 / (1.0 + jnp.exp(-x))


def _silu(x):
    return x * _sigmoid(x)


def _softplus(x):
    return jnp.maximum(x, 0.0) + jnp.log1p(jnp.exp(-jnp.abs(x)))


def _rmsnorm_rows(x, g):
    return x * lax.rsqrt(jnp.mean(x * x, axis=-1, keepdims=True) + EPS) * g


def _tri(n, strict=False):
    r = lax.broadcasted_iota(jnp.int32, (n, n), 0)
    c = lax.broadcasted_iota(jnp.int32, (n, n), 1)
    return (c < r) if strict else (c <= r)


def _ffn_kernel(h_ref, g_ref, wg_ref, wu_ref, wd_ref, fg_ref, o_ref, xn_s, act_s, *, tf, final):
    xn_s[...] = _rmsnorm_rows(h_ref[...], g_ref[...]).astype(xn_s.dtype)
    F = wg_ref.shape[1]
    for f in range(F // tf):
        cols = slice(f * tf, (f + 1) * tf)
        xn = xn_s[...]
        gate = jnp.dot(xn, wg_ref[:, cols], preferred_element_type=F32)
        up = jnp.dot(xn, wu_ref[:, cols], preferred_element_type=F32)
        act_s[:, cols] = (_silu(gate) * up).astype(act_s.dtype)
    out = h_ref[...] + 0.5 * jnp.dot(act_s[...], wd_ref[...], preferred_element_type=F32)
    if final:
        out = _rmsnorm_rows(out, fg_ref[...])
    o_ref[...] = out


def _resident(shape):
    return pl.BlockSpec(shape, lambda *_: (0,) * len(shape), pipeline_mode=pl.Buffered(1))


def _layer_block(w, l):
    return pl.BlockSpec((None,) + w.shape[1:], lambda *_: (l,) + (0,) * (w.ndim - 1),
                        pipeline_mode=pl.Buffered(1))


def _ffn(h, g, wg, wu, wd, fg, l, *, final, tm=256, tf=256):
    T, D = h.shape
    F = wg.shape[2]
    tm = min(tm, T)
    return pl.pallas_call(
        functools.partial(_ffn_kernel, tf=tf, final=final),
        grid=(T // tm,),
        in_specs=[
            pl.BlockSpec((tm, D), lambda i: (i, 0)),
            _resident((1, D)),
            _layer_block(wg, l),
            _layer_block(wu, l),
            _layer_block(wd, l),
            _resident((1, D)),
        ],
        out_specs=pl.BlockSpec((tm, D), lambda i: (i, 0)),
        out_shape=jax.ShapeDtypeStruct((T, D), F32),
        scratch_shapes=[pltpu.VMEM((tm, D), MXU_DTYPE), pltpu.VMEM((tm, F), MXU_DTYPE)],
        compiler_params=_cparams(("parallel",)),
        name="ffn",
    )(h, g.reshape(1, D), wg, wu, wd, fg.reshape(1, D))


def _proj_kernel(h_ref, g_ref, w_ref, o_ref):
    xn = _rmsnorm_rows(h_ref[...], g_ref[...]).astype(w_ref.dtype)
    o_ref[...] = jnp.dot(xn, w_ref[...], preferred_element_type=F32).astype(o_ref.dtype)


def _proj(h, g, w, l, out_dtype, *, tm=512):
    T, D = h.shape
    N = w.shape[2]
    tm = min(tm, T)
    return pl.pallas_call(
        _proj_kernel,
        grid=(T // tm,),
        in_specs=[
            pl.BlockSpec((tm, D), lambda i: (i, 0)),
            pl.BlockSpec((1, D), lambda i: (0, 0)),
            _layer_block(w, l),
        ],
        out_specs=pl.BlockSpec((tm, N), lambda i: (i, 0)),
        out_shape=jax.ShapeDtypeStruct((T, N), out_dtype),
        compiler_params=_cparams(("parallel",)),
        name="proj",
    )(h, g.reshape(1, D), w)


def _proj_da_kernel(h_ref, g_ref, w_ref, cos_ref, sin_ref, o_ref, *, nqk):
    xn = _rmsnorm_rows(h_ref[...], g_ref[...]).astype(w_ref.dtype)
    p = jnp.dot(xn, w_ref[...], preferred_element_type=F32)
    cos = cos_ref[...]
    sin = sin_ref[...]
    tm = p.shape[0]
    lane = lax.broadcasted_iota(jnp.int32, (tm, LANE), 1)
    first_half = (lane % DA_QK) < (DA_QK // 2)
    qscale = DA_QK ** -0.5 * math.log2(math.e)
    for blk in range(2 * nqk):
        t = p[:, blk * LANE:(blk + 1) * LANE]
        swapped = jnp.where(first_half, pltpu.roll(t, LANE - DA_QK // 2, 1), pltpu.roll(t, DA_QK // 2, 1))
        t = t * cos + swapped * sin
        if blk < nqk:
            t = t * qscale
        o_ref[:, blk * LANE:(blk + 1) * LANE] = t.astype(o_ref.dtype)
    ones = jnp.ones((tm, LANE), o_ref.dtype)
    for hd in range(DA_HEADS):
        src = (2 * nqk + hd) * LANE
        dst = (2 * nqk + 2 * hd) * LANE
        o_ref[:, dst:dst + LANE] = p[:, src:src + LANE].astype(o_ref.dtype)
        o_ref[:, dst + LANE:dst + 2 * LANE] = ones


def _proj_da(h, g, w, l, cos_t, sin_t, S, *, tm=512):
    T, D = h.shape
    N = w.shape[2] + DA_HEADS * LANE
    tm = min(tm, S)
    ns = S // tm
    nqk = DA_HEADS * 2 * DA_QK // LANE
    return pl.pallas_call(
        functools.partial(_proj_da_kernel, nqk=nqk),
        grid=(T // tm,),
        in_specs=[
            pl.BlockSpec((tm, D), lambda i: (i, 0)),
            pl.BlockSpec((1, D), lambda i: (0, 0)),
            _layer_block(w, l),
            pl.BlockSpec((tm, LANE), lambda i: (i % ns, 0)),
            pl.BlockSpec((tm, LANE), lambda i: (i % ns, 0)),
        ],
        out_specs=pl.BlockSpec((tm, N), lambda i: (i, 0)),
        out_shape=jax.ShapeDtypeStruct((T, N), MXU_DTYPE),
        compiler_params=_cparams(("parallel",)),
        name="proj_da",
    )(h, g.reshape(1, D), w, cos_t, sin_t)


def _da_kernel(lam_ref, q_ref, k_ref, v_ref, g_ref, o_ref, q_s, m_s, l_s, acc_s, *, tq, nsub, hpb, out_scale):
    qi = pl.program_id(2)
    lane = lax.broadcasted_iota(jnp.int32, (tq, LANE), 1)
    for hd in range(hpb):
        q = q_ref[:, hd * LANE:(hd + 1) * LANE]
        zero = jnp.zeros_like(q)
        q_s[2 * hd] = jnp.where(lane < DA_QK, q, zero)
        q_s[2 * hd + 1] = jnp.where(lane >= DA_QK, q, zero)
    m_s[...] = jnp.full_like(m_s, -jnp.inf)
    l_s[...] = jnp.zeros_like(l_s)
    acc_s[...] = jnp.zeros_like(acc_s)
    tr = tq // nsub
    reps = tq // LANE
    chains = [(2 * hd + c, hd, pl.ds(r * tr, tr), r) for hd in range(hpb) for r in range(nsub) for c in range(2)]

    def step(j, masked):
        rows = pl.ds(pl.multiple_of(j * tq, tq), tq)
        k = [k_ref[rows, hd * LANE:(hd + 1) * LANE] for hd in range(hpb)]
        v = [v_ref[rows, hd * 2 * LANE:(hd + 1) * 2 * LANE] for hd in range(hpb)]
        s = [lax.dot_general(q_s[n, rs, :], k[hd], (((1,), (1,)), ((), ())), preferred_element_type=F32)
             for n, hd, rs, _ in chains]
        if masked:
            col = lax.broadcasted_iota(jnp.int32, (tr, tq), 1) // CHUNK
            row = lax.broadcasted_iota(jnp.int32, (tr, tq), 0)
            s = [jnp.where(col <= (row + ch[3] * tr) // CHUNK, x, -jnp.inf) for x, ch in zip(s, chains)]
        m_prev = [m_s[n, rs, :] for n, _, rs, _ in chains]
        m_new = [jnp.maximum(mp, jnp.max(x, axis=-1, keepdims=True)) for mp, x in zip(m_prev, s)]
        p = [jnp.exp2(x - jnp.concatenate([mn] * reps, axis=1)) for x, mn in zip(s, m_new)]
        alpha = [jnp.exp2(mp - mn) for mp, mn in zip(m_prev, m_new)]
        pv = [jnp.dot(x.astype(MXU_DTYPE), v[ch[1]], preferred_element_type=F32) for x, ch in zip(p, chains)]
        for (n, _, rs, _), mn, a, o in zip(chains, m_new, alpha, pv):
            m_s[n, rs, :] = mn
            l_s[n, rs, :] = a * l_s[n, rs, :] + o[:, DA_V:]
            acc_s[n, rs, :] = a * acc_s[n, rs, :] + o[:, :DA_V]

    def block_pair(i, carry):
        step(2 * i, False)
        step(2 * i + 1, False)
        return carry

    lax.fori_loop(0, qi // 2, block_pair, 0)

    @pl.when(qi % 2 == 1)
    def _():
        step(qi - 1, False)

    step(qi, True)
    lam = lam_ref[0, 0]
    for hd in range(hpb):
        o = acc_s[2 * hd] / l_s[2 * hd] - lam * (acc_s[2 * hd + 1] / l_s[2 * hd + 1])
        o = _rmsnorm_rows(o, g_ref[...]) * out_scale
        o_ref[:, hd * DA_V:(hd + 1) * DA_V] = o.astype(o_ref.dtype)


def _diff_attention(lam, pda, da_norm, B, S, lam_init, *, tq=512, nsub=2, hpb=2):
    T = pda.shape[0]
    tq = min(tq, S)
    nq = S // tq
    HB = DA_HEADS // hpb
    kernel = functools.partial(_da_kernel, tq=tq, nsub=nsub, hpb=hpb, out_scale=1.0 - lam_init)
    return pl.pallas_call(
        kernel,
        grid=(B, HB, nq),
        in_specs=[
            pl.BlockSpec(memory_space=pltpu.SMEM),
            pl.BlockSpec((tq, hpb * LANE), lambda b, h, i: (b * nq + i, h)),
            pl.BlockSpec((S, hpb * LANE), lambda b, h, i: (b, HB + h)),
            pl.BlockSpec((S, hpb * 2 * LANE), lambda b, h, i: (b, HB + h)),
            pl.BlockSpec((1, LANE), lambda b, h, i: (0, 0)),
        ],
        out_specs=pl.BlockSpec((tq, hpb * DA_V), lambda b, h, i: (b * nq + i, h)),
        out_shape=jax.ShapeDtypeStruct((T, DA_HEADS * DA_V), MXU_DTYPE),
        scratch_shapes=[
            pltpu.VMEM((2 * hpb, tq, LANE), MXU_DTYPE),
            pltpu.VMEM((2 * hpb, tq, LANE), F32),
            pltpu.VMEM((2 * hpb, tq, LANE), F32),
            pltpu.VMEM((2 * hpb, tq, DA_V), F32),
        ],
        compiler_params=_cparams(("parallel", "parallel", "arbitrary")),
        name="diff_attn",
    )(lam, pda, pda, pda, da_norm.reshape(1, DA_V))


def _gla_kernel(p_ref, w2_ref, gb_ref, ng_ref, o_ref, st_s, lg_s, *, tc, nb, ahead):
    i = pl.program_id(0)
    HK = GLA_HEADS * GLA_DK
    HV = GLA_HEADS * GLA_DV

    @pl.when(i == 0)
    def _():
        st_s[...] = jnp.zeros_like(st_s)

    for b in range(nb):
        z = _mm(p_ref[b, :, 2 * HK + 2 * HV:], w2_ref[...]) + gb_ref[...]
        lg_s[b] = (jnp.minimum(z, 0.0) - jnp.log1p(jnp.exp(-jnp.abs(z)))) * (1.0 / GLA_TAU)

    tri = _tri(CHUNK).astype(F32)
    causal = _tri(CHUNK)
    lane_k = lax.broadcasted_iota(jnp.int32, (CHUNK, HK), 1) // GLA_DK
    bd = (lax.broadcasted_iota(jnp.int32, (HV, HK), 0) // GLA_DV
          == lax.broadcasted_iota(jnp.int32, (HV, HK), 1) // GLA_DK)
    ng = ng_ref[...]

    def body(c, carry):
        inst = [(b, pl.ds(pl.multiple_of((c * ahead + u) * CHUNK, CHUNK), CHUNK))
                for u in range(ahead) for b in range(nb)]
        q = [p_ref[b, rows, 0:HK] * (GLA_DK ** -0.5) for b, rows in inst]
        k = [p_ref[b, rows, HK:2 * HK] for b, rows in inst]
        v = [p_ref[b, rows, 2 * HK:2 * HK + HV].astype(MXU_DTYPE) for b, rows in inst]
        G = [_sel_l(tri, lg_s[b, rows, :]) for b, rows in inst]
        g_mid = [x[CHUNK // 2:CHUNK // 2 + 1] for x in G]
        g_last = [x[CHUNK - 1:CHUNK] for x in G]
        qe = [x * jnp.exp(g - gm) for x, g, gm in zip(q, G, g_mid)]
        ke = [(x * jnp.exp(gm - g)).astype(MXU_DTYPE) for x, g, gm in zip(k, G, g_mid)]
        q_in = [x * jnp.exp(g) for x, g in zip(q, G)]
        k_end = [x * jnp.exp(gl - g) for x, g, gl in zip(k, G, g_last)]
        att = [[jnp.where(causal, _mm_nt(jnp.where(lane_k == h, x, 0.0), y), 0.0) for x, y in zip(qe, ke)]
               for h in range(GLA_HEADS)]
        y_intra = [[_mm(a, x[:, h * GLA_DV:(h + 1) * GLA_DV]) for a, x in zip(att[h], v)]
                   for h in range(GLA_HEADS)]
        kv = [jnp.where(bd, _mm_tn(x, y), 0.0) for x, y in zip(v, k_end)]
        for n, (b, rows) in enumerate(inst):
            st = st_s[b]
            y_inter = _mm_nt(q_in[n], st)
            st_s[b] = st * jnp.exp(g_last[n]) + kv[n]
            outs = [_rmsnorm_rows(y_intra[h][n] + y_inter[:, h * GLA_DV:(h + 1) * GLA_DV], ng)
                    for h in range(GLA_HEADS)]
            og = p_ref[b, rows, 2 * HK + HV:2 * HK + 2 * HV]
            o_ref[b, rows, :] = (jnp.concatenate(outs, axis=1) * _silu(og)).astype(o_ref.dtype)
        return carry

    lax.fori_loop(0, tc // (CHUNK * ahead), body, 0)


def _gla(pg, w2pad, gate_b, gla_norm, B, S, *, tc=512, ahead=2):
    T, N = pg.shape
    tc = min(tc, S)
    ns = S // tc
    HK, HV = GLA_HEADS * GLA_DK, GLA_HEADS * GLA_DV
    const = lambda i: (0, 0)
    out = pl.pallas_call(
        functools.partial(_gla_kernel, tc=tc, nb=B, ahead=ahead),
        grid=(ns,),
        in_specs=[
            pl.BlockSpec((B, tc, N), lambda i: (0, i, 0)),
            pl.BlockSpec((LANE, HK), const),
            pl.BlockSpec((1, HK), const),
            pl.BlockSpec((1, GLA_DV), const),
        ],
        out_specs=pl.BlockSpec((B, tc, HV), lambda i: (0, i, 0)),
        out_shape=jax.ShapeDtypeStruct((B, S, HV), MXU_DTYPE),
        scratch_shapes=[pltpu.VMEM((B, HV, HK), F32), pltpu.VMEM((B, tc, HK), F32)],
        compiler_params=_cparams(("arbitrary",)),
        name="gla",
    )(pg.reshape(B, S, N), w2pad, gate_b.reshape(1, HK), gla_norm.reshape(1, GLA_DV))
    return out.reshape(T, HV)


def _ssd_kernel(p_ref, cw_ref, cb_ref, dtb_ref, a_ref, e64_ref, e128_ref, dx_ref, ng_ref, o_ref,
                st_s, xpad_s, xs_s, bc_s, ax_s, ax2_s, xdt_s, *, tc, nb, ahead):
    i = pl.program_id(0)
    NI = SSD_INNER
    GN = SSD_GROUPS * SSD_N
    CD = NI + 2 * GN
    PAD = SUBLANE

    @pl.when(i == 0)
    def _():
        st_s[...] = jnp.zeros_like(st_s)
        xpad_s[:, 0:PAD, :] = jnp.zeros((nb, PAD, CD), F32)

    for b in range(nb):
        xpad_s[b, PAD:PAD + tc, :] = p_ref[b, :, NI:NI + CD]
        conv = cb_ref[...]
        for kk in range(SSD_CONV):
            off = PAD - (SSD_CONV - 1) + kk
            conv = conv + cw_ref[kk:kk + 1, :] * xpad_s[b, off:off + tc, :]
        xpad_s[b, 0:PAD, :] = xpad_s[b, tc:tc + PAD, :]
        xbc = _silu(conv)
        xs_s[b] = xbc[:, 0:NI]
        bc_s[b] = xbc[:, NI:]
        dt = _softplus(p_ref[b, :, NI + CD:] + dtb_ref[...])
        dta = dt * a_ref[...]
        ax_s[b] = _sel_r(dta, e64_ref[...])
        ax2_s[b] = _sel_r(dta, e128_ref[...])
        xdt_s[b] = xbc[:, 0:NI] * _sel_r(dt, e64_ref[...])

    tri = _tri(CHUNK).astype(F32)
    tri_t = (lax.broadcasted_iota(jnp.int32, (CHUNK, CHUNK), 0)
             <= lax.broadcasted_iota(jnp.int32, (CHUNK, CHUNK), 1)).astype(F32)
    causal = _tri(CHUNK)
    low_half = lax.broadcasted_iota(jnp.int32, (CHUNK, LANE), 1) < SSD_P
    rep = SSD_HEADS // SSD_GROUPS
    GW = rep * SSD_P

    def body(c, carry):
        inst = [(b, pl.ds(pl.multiple_of((c * ahead + u) * CHUNK, CHUNK), CHUNK))
                for u in range(ahead) for b in range(nb)]
        xdt = [xdt_s[b, rows, :] for b, rows in inst]
        acx = [_sel_l(tri, ax_s[b, rows, :]) for b, rows in inst]
        acx2 = [_sel_l(tri, ax2_s[b, rows, :]) for b, rows in inst]
        dta2 = [ax2_s[b, rows, :] for b, rows in inst]
        a_last = [x[CHUNK - 1:CHUNK] for x in acx]
        x_end = [(x * jnp.exp(al - a)).astype(MXU_DTYPE) for x, a, al in zip(xdt, acx, a_last)]
        xdt_m = [x.astype(MXU_DTYPE) for x in xdt]
        bm = [[bc_s[b, rows, g * SSD_N:(g + 1) * SSD_N].astype(MXU_DTYPE) for b, rows in inst]
              for g in range(SSD_GROUPS)]
        cm = [[bc_s[b, rows, GN + g * SSD_N:GN + (g + 1) * SSD_N].astype(MXU_DTYPE) for b, rows in inst]
              for g in range(SSD_GROUPS)]
        cb = [[_mm_nt(x, y) for x, y in zip(cm[g], bm[g])] for g in range(SSD_GROUPS)]
        new_st = [jnp.concatenate([_mm_tn(bm[g][n], x_end[n][:, g * GW:(g + 1) * GW])
                                   for g in range(SSD_GROUPS)], axis=1) for n in range(len(inst))]
        y_intra = []
        for n in range(len(inst)):
            parts = []
            for pair in range(SSD_HEADS // 2):
                res = []
                for hh in range(2):
                    h = pair * 2 + hh
                    col = acx2[n][:, h * LANE:h * LANE + CHUNK]
                    row = jnp.sum(tri_t * dta2[n][:, h * LANE:h * LANE + CHUNK], axis=0, keepdims=True)
                    lmat = jnp.exp(jnp.where(causal, col - row, -jnp.inf))
                    res.append(_mm(cb[h // rep][n] * lmat, xdt_m[n][:, pair * LANE:(pair + 1) * LANE]))
                parts.append(jnp.where(low_half, res[0], res[1]))
            y_intra.append(jnp.concatenate(parts, axis=1))
        half = NI // SSD_GROUPS
        for n, (b, rows) in enumerate(inst):
            st = st_s[b]
            inter = jnp.concatenate([_mm(cm[g][n], st[:, g * GW:(g + 1) * GW]) for g in range(SSD_GROUPS)],
                                    axis=1)
            st_s[b] = st * jnp.exp(a_last[n]) + new_st[n]
            y = y_intra[n] + inter * jnp.exp(acx[n]) + dx_ref[...] * xs_s[b, rows, :]
            y = y * _silu(p_ref[b, rows, 0:NI])
            outs = [_rmsnorm_rows(y[:, g * half:(g + 1) * half], ng_ref[:, g * half:(g + 1) * half])
                    for g in range(SSD_GROUPS)]
            o_ref[b, rows, :] = jnp.concatenate(outs, axis=1).astype(o_ref.dtype)
        return carry

    lax.fori_loop(0, tc // (CHUNK * ahead), body, 0)


def _ssd(ps, conv_w, conv_b, dtb_pad, a_pad, e64, e128, d_x, ssd_norm, B, S, *, tc=512, ahead=2):
    T, N = ps.shape
    tc = min(tc, S)
    ns = S // tc
    NI = SSD_INNER
    CD = NI + 2 * SSD_GROUPS * SSD_N
    const = lambda i: (0, 0)
    out = pl.pallas_call(
        functools.partial(_ssd_kernel, tc=tc, nb=B, ahead=ahead),
        grid=(ns,),
        in_specs=[
            pl.BlockSpec((B, tc, N), lambda i: (0, i, 0)),
            pl.BlockSpec((SSD_CONV, CD), const),
            pl.BlockSpec((1, CD), const),
            pl.BlockSpec((1, LANE), const),
            pl.BlockSpec((1, LANE), const),
            pl.BlockSpec((LANE, NI), const),
            pl.BlockSpec((LANE, SSD_HEADS * LANE), const),
            pl.BlockSpec((1, NI), const),
            pl.BlockSpec((1, NI), const),
        ],
        out_specs=pl.BlockSpec((B, tc, NI), lambda i: (0, i, 0)),
        out_shape=jax.ShapeDtypeStruct((B, S, NI), MXU_DTYPE),
        scratch_shapes=[
            pltpu.VMEM((B, SSD_N, NI), F32),
            pltpu.VMEM((B, tc + SUBLANE, CD), F32),
            pltpu.VMEM((B, tc, NI), F32),
            pltpu.VMEM((B, tc, 2 * SSD_GROUPS * SSD_N), F32),
            pltpu.VMEM((B, tc, NI), F32),
            pltpu.VMEM((B, tc, SSD_HEADS * LANE), F32),
            pltpu.VMEM((B, tc, NI), F32),
        ],
        compiler_params=_cparams(("arbitrary",)),
        name="ssd",
    )(ps.reshape(B, S, N), conv_w, conv_b.reshape(1, CD), dtb_pad, a_pad, e64, e128, d_x,
      ssd_norm.reshape(1, NI))
    return out.reshape(T, NI)


def _unit_lower_inverse(mats, eye, same_blk, stack):
    d = [jnp.where(same_blk, a, 0.0) for a in mats]
    o = [a - x for a, x in zip(mats, d)]
    t = [eye + x for x in d]
    dp = d
    sd = [stack(x) for x in dp]
    for _ in range(3):
        dp = [_mm(x, y) for x, y in zip(dp, sd)]
        sd = [stack(x) for x in dp]
        t = [x + _mm(x, y) for x, y in zip(t, sd)]
    n = [_mm(x, stack(y)) for x, y in zip(t, o)]
    n2 = [_mm(x, stack(x)) for x in n]
    q = [eye + x for x in n]
    q = [x + _mm(x, stack(y)) for x, y in zip(q, n2)]
    return [_mm(x, stack(y)) for x, y in zip(q, t)]


def _rw_kernel(p_ref, mu_ref, w0_ref, w2_ref, a0_ref, a2_ref, g2_ref, kk_ref, ka_ref, rk_ref,
               nw_ref, nb_ref, bd_ref, o_ref,
               mt_s, ppad_s, r_s, k_s, v_s, lw_s, al_s, be_s, g_s, *, tc, nb, ahead):
    i = pl.program_id(0)
    R = RW_DIM
    PAD = SUBLANE
    NP = p_ref.shape[2]
    NPAIR = RW_HEADS // 2

    @pl.when(i == 0)
    def _():
        mt_s[...] = jnp.zeros_like(mt_s)
        ppad_s[:, 0:PAD, :] = jnp.zeros((nb, PAD, NP), F32)

    bd = bd_ref[...]
    for b in range(nb):
        p = p_ref[b]
        ppad_s[b, PAD:PAD + tc, :] = p
        prev = ppad_s[b, PAD - 1:PAD - 1 + tc, :]
        ppad_s[b, 0:PAD, :] = ppad_s[b, tc:tc + PAD, :]
        pm = p + (prev - p) * mu_ref[...]
        r = pm[:, 0:R]
        k = pm[:, R:2 * R]
        v = pm[:, 2 * R:3 * R]
        wa = pm[:, 3 * R:3 * R + LANE]
        gl = pm[:, 3 * R + LANE:3 * R + 2 * LANE]
        lw_s[b] = -RW_DECAY_SCALE * _sigmoid(w0_ref[...] + _mm(jnp.tanh(wa), w2_ref[...]))
        a = _sigmoid(a0_ref[...] + _mm(wa, a2_ref[...]))
        g_s[b] = _mm(_sigmoid(gl), g2_ref[...])
        kk = k * kk_ref[...]
        kk = kk / jnp.maximum(jnp.sqrt(_sel_r(kk * kk, bd)), 1e-12)
        r_s[b] = r
        k_s[b] = k * (1.0 + (a - 1.0) * ka_ref[...])
        v_s[b] = v
        al_s[b] = -kk
        be_s[b] = kk * a

    C2 = 2 * CHUNK
    tri = _tri(CHUNK).astype(F32)
    tt = lax.broadcasted_iota(jnp.int32, (CHUNK, C2), 0)
    ss = lax.broadcasted_iota(jnp.int32, (CHUNK, C2), 1) % CHUNK
    strict = ss < tt
    incl = ss <= tt
    eye = (ss == tt).astype(F32)
    same_blk = (tt // 16) == (ss // 16)
    m0 = lax.broadcasted_iota(jnp.int32, (CHUNK, LANE), 1) < RW_HEAD
    bdm = (lax.broadcasted_iota(jnp.int32, (C2, C2), 0) // CHUNK
           == lax.broadcasted_iota(jnp.int32, (C2, C2), 1) // CHUNK)

    def stack(x):
        xb = x.astype(MXU_DTYPE)
        zero = jnp.zeros_like(xb)
        return jnp.concatenate([jnp.where(m0, xb, zero), jnp.where(m0, zero, xb)], axis=0)

    def body(c, carry):
        row_sl = [pl.ds(pl.multiple_of((c * ahead + u) * CHUNK, CHUNK), CHUNK) for u in range(ahead)]
        w_end, vv_b, wide = [], [], []
        for rows in row_sl:
            for b in range(nb):
                lw = lw_s[b, rows, :]
                cum = _sel_l(tri, lw)
                e_cum = jnp.exp(cum)
                e_inv = jnp.exp(-cum)
                ah = al_s[b, rows, :] * jnp.exp(cum - lw)
                bh = be_s[b, rows, :] * e_inv
                kh = k_s[b, rows, :] * e_inv
                rh = r_s[b, rows, :] * e_cum
                vv = v_s[b, rows, :]
                vv_b.append(vv)
                for j in range(NPAIR):
                    sl = slice(j * LANE, (j + 1) * LANE)
                    w_end.append(e_cum[CHUNK - 1:CHUNK, sl])
                    wide.append(tuple(x[:, sl].astype(MXU_DTYPE) for x in (ah, rh, bh, kh, vv)))
        nch = nb * NPAIR
        ar = [jnp.concatenate([w[0], w[1]], axis=0) for w in wide]
        bk = [jnp.concatenate([stack(w[2]), stack(w[3])], axis=0) for w in wide]
        v_bd = [stack(w[4]) for w in wide]
        big = [_mm_nt(x, y) for x, y in zip(ar, bk)]
        a_ab = [jnp.where(strict, x[0:CHUNK, 0:C2], 0.0) for x in big]
        a_ak = [jnp.where(strict, x[0:CHUNK, C2:], 0.0) for x in big]
        r_b = [jnp.where(incl, x[CHUNK:, 0:C2], 0.0) for x in big]
        r_k = [jnp.where(incl, x[CHUNK:, C2:], 0.0) for x in big]
        t_inv = _unit_lower_inverse(a_ab, eye, same_blk, stack)
        for u, rows in enumerate(row_sl):
            ch = slice(u * nch, (u + 1) * nch)
            mt = [mt_s[n] for n in range(nch)]
            rhs = [_mm_nt(w[0], m) + _mm(x, v) for w, m, x, v in zip(wide[ch], mt, a_ak[ch], v_bd[ch])]
            u_w = [_mm(x, stack(y)) for x, y in zip(t_inv[ch], rhs)]
            upd = [_mm_tn(jnp.concatenate([x.astype(MXU_DTYPE), w[4]], axis=0),
                          jnp.concatenate([w[2], w[3]], axis=0)) for x, w in zip(u_w, wide[ch])]
            for n in range(nch):
                mt_s[n] = (mt[n] + jnp.where(bdm, upd[n], 0.0)) * w_end[u * nch + n]
            y_w = [_mm_nt(w[1], m) + _mm(x, stack(uu)) + _mm(z, v)
                   for w, m, x, uu, z, v in zip(wide[ch], mt, r_b[ch], u_w, r_k[ch], v_bd[ch])]
            for b in range(nb):
                y = jnp.concatenate([y_w[b * NPAIR + j] for j in range(NPAIR)], axis=1)
                mean = _sel_r(y, bd) * (1.0 / RW_HEAD)
                yc = y - mean
                var = _sel_r(yc * yc, bd) * (1.0 / RW_HEAD)
                yn = yc * lax.rsqrt(var + RW_GN_EPS) * nw_ref[...] + nb_ref[...]
                bonus = _sel_r(r_s[b, rows, :] * k_s[b, rows, :] * rk_ref[...], bd) * vv_b[u * nb + b]
                o_ref[b, rows, :] = ((yn + bonus) * g_s[b, rows, :]).astype(o_ref.dtype)
        return carry

    lax.fori_loop(0, tc // (CHUNK * ahead), body, 0)


def _rwkv(pr, mu, w0, w2pad, a0, a2pad, g2, k_k, k_a, r_k, norm_w, norm_b, bd, B, S, *, tc=256, ahead=4):
    T, N = pr.shape
    tc = min(tc, S)
    ns = S // tc
    R = RW_DIM
    const = lambda i: (0, 0)
    row = lambda x: x.reshape(1, -1)
    vec = pl.BlockSpec((1, R), const)
    out = pl.pallas_call(
        functools.partial(_rw_kernel, tc=tc, nb=B, ahead=ahead),
        grid=(ns,),
        in_specs=[
            pl.BlockSpec((B, tc, N), lambda i: (0, i, 0)),
            pl.BlockSpec((1, N), const),
            vec,
            pl.BlockSpec((LANE, R), const),
            vec,
            pl.BlockSpec((LANE, R), const),
            pl.BlockSpec((LANE, R), const),
            vec, vec, vec, vec, vec,
            pl.BlockSpec((R, R), const),
        ],
        out_specs=pl.BlockSpec((B, tc, R), lambda i: (0, i, 0)),
        out_shape=jax.ShapeDtypeStruct((B, S, R), MXU_DTYPE),
        scratch_shapes=[
            pltpu.VMEM((B * RW_HEADS // 2, LANE, LANE), F32),
            pltpu.VMEM((B, tc + SUBLANE, N), F32),
        ] + [pltpu.VMEM((B, tc, R), F32) for _ in range(7)],
        compiler_params=_cparams(("arbitrary",)),
        name="rwkv7",
    )(pr.reshape(B, S, N), row(mu), row(w0), w2pad, row(a0), a2pad, g2, row(k_k), row(k_a), row(r_k),
      row(norm_w), row(norm_b), bd)
    return out.reshape(T, R)


def _merge_kernel(h_ref, g_ref, wgate_ref, gb_ref, ya_ref, yb_ref, yc_ref, yd_ref, wb_ref, wo_ref, o_ref):
    h = h_ref[...]
    D = h.shape[1]
    xn = _rmsnorm_rows(h, g_ref[...]).astype(wgate_ref.dtype)
    merged = jnp.zeros(h.shape, F32)
    for n, y_ref in enumerate((ya_ref, yb_ref, yc_ref, yd_ref)):
        logits = jnp.dot(xn, wgate_ref[:, n * D:(n + 1) * D], preferred_element_type=F32) + gb_ref[n:n + 1, :]
        proj = jnp.dot(y_ref[...], wb_ref[n], preferred_element_type=F32)
        merged = merged + _sigmoid(logits) * proj
    o_ref[...] = h + jnp.dot(merged.astype(wo_ref.dtype), wo_ref[...], preferred_element_type=F32)


def _merge(h, g, wgate, gate_b, ys, w_branch, w_out, l, *, tm=256):
    T, D = h.shape
    tm = min(tm, T)
    _, NB, MW, _ = w_branch.shape
    const2 = lambda i: (0, 0)
    ytile = pl.BlockSpec((tm, MW), lambda i: (i, 0))
    return pl.pallas_call(
        _merge_kernel,
        grid=(T // tm,),
        in_specs=[
            pl.BlockSpec((tm, D), lambda i: (i, 0)),
            pl.BlockSpec((1, D), const2),
            _layer_block(wgate, l),
            pl.BlockSpec((NB, D), const2),
            ytile, ytile, ytile, ytile,
            _layer_block(w_branch, l),
            _layer_block(w_out, l),
        ],
        out_specs=pl.BlockSpec((tm, D), lambda i: (i, 0)),
        out_shape=jax.ShapeDtypeStruct((T, D), F32),
        compiler_params=_cparams(("parallel",)),
        name="merge",
    )(h, g.reshape(1, D), wgate, gate_b, *ys, w_branch, w_out)


def _pad_cols(w, n):
    return jnp.pad(w, ((0, 0), (0, n - w.shape[1])))


def _pad_rows_at(w, start, total):
    return jnp.pad(w, ((start, total - start - w.shape[0]), (0, 0)))


def _rope_tables(S):
    half = DA_QK // 2
    inv_freq = ROPE_THETA ** (-jnp.arange(half, dtype=F32) / half)
    ang = jnp.arange(S, dtype=F32)[:, None] * inv_freq[None, :]
    cos, sin = jnp.cos(ang), jnp.sin(ang)
    reps = LANE // DA_QK
    return jnp.tile(jnp.concatenate([cos, cos], axis=1), (1, reps)), jnp.tile(jnp.concatenate([-sin, sin], axis=1), (1, reps))


def _head_expand(heads, width):
    r = jnp.arange(LANE)[:, None]
    c = jnp.arange(heads * width)[None, :] // width
    return (r == c).astype(F32)


def kernel(x, ffn1_norm, ffn1_wg, ffn1_wu, ffn1_wd, mix_norm, w_in, da_lambda_q1, da_lambda_k1, da_lambda_q2, da_lambda_k2, da_norm, gla_gate_w2, gla_gate_b, gla_norm, ssd_conv_w, ssd_conv_b, ssd_dt_bias, ssd_a_log, ssd_d, ssd_norm, rw_mu, rw_w0, rw_w2, rw_a0, rw_a2, rw_g2, rw_k_k, rw_k_a, rw_r_k, rw_norm_w, rw_norm_b, w_branch, gate_b, w_out, ffn2_norm, ffn2_wg, ffn2_wu, ffn2_wd, final_norm):
    B, S, D = x.shape
    depth = w_in.shape[0]
    T = B * S
    h = x.reshape(T, D)
    cos_t, sin_t = _rope_tables(S)
    e64 = _head_expand(SSD_HEADS, SSD_P)
    e128 = _head_expand(SSD_HEADS, LANE)
    rw_bd = (jnp.arange(RW_DIM)[:, None] // RW_HEAD == jnp.arange(RW_DIM)[None, :] // RW_HEAD).astype(F32)

    n_da = DA_HEADS * (4 * DA_QK + DA_V)
    n_gla = 2 * GLA_HEADS * GLA_DK + 2 * GLA_HEADS * GLA_DV + GLA_RANK
    n_ssd = SSD_INNER + (SSD_INNER + 2 * SSD_GROUPS * SSD_N) + SSD_HEADS
    n_rw = 3 * RW_DIM + rw_w2.shape[1] + rw_a2.shape[1] + rw_g2.shape[1]
    o_gla, o_ssd, o_rw, o_gate = n_da, n_da + n_gla, n_da + n_gla + n_ssd, n_da + n_gla + n_ssd + n_rw
    cast = lambda w: w.astype(MXU_DTYPE)
    pad_last = lambda w: jnp.pad(w, ((0, 0), (0, 0), (0, -w.shape[2] % LANE)))
    f1g, f1u, f1d = cast(ffn1_wg), cast(ffn1_wu), cast(ffn1_wd)
    f2g, f2u, f2d = cast(ffn2_wg), cast(ffn2_wu), cast(ffn2_wd)
    w_da = cast(w_in[:, :, 0:n_da])
    w_gla = cast(pad_last(w_in[:, :, o_gla:o_ssd]))
    w_ssd = cast(pad_last(w_in[:, :, o_ssd:o_rw]))
    w_rw = cast(w_in[:, :, o_rw:o_gate])
    w_gate = cast(w_in[:, :, o_gate:])
    wb, wo = cast(w_branch), cast(w_out)

    for l in range(depth):
        h = _ffn(h, ffn1_norm[l], f1g, f1u, f1d, final_norm, l, final=False)
        g = mix_norm[l]

        lam_init = 0.8 - 0.6 * math.exp(-0.3 * l)
        lam = (jnp.exp(jnp.sum(da_lambda_q1[l] * da_lambda_k1[l])) - jnp.exp(jnp.sum(da_lambda_q2[l] * da_lambda_k2[l]))
               + lam_init).reshape(1, 1).astype(F32)
        pda = _proj_da(h, g, w_da, l, cos_t, sin_t, S)
        y_a = _diff_attention(lam, pda, da_norm[l], B, S, lam_init)

        pg = _proj(h, g, w_gla, l, F32)
        y_b = _gla(pg, _pad_rows_at(gla_gate_w2[l], 0, LANE), gla_gate_b[l], gla_norm[l], B, S)

        ps = _proj(h, g, w_ssd, l, F32)
        dtb_pad = _pad_cols(ssd_dt_bias[l].reshape(1, -1), LANE)
        a_pad = _pad_cols(-jnp.exp(ssd_a_log[l]).reshape(1, -1), LANE)
        d_x = jnp.repeat(ssd_d[l], SSD_P).reshape(1, -1)
        y_c = _ssd(ps, ssd_conv_w[l], ssd_conv_b[l], dtb_pad, a_pad, e64, e128, d_x, ssd_norm[l], B, S)

        pr = _proj(h, g, w_rw, l, F32)
        rank_w = rw_w2.shape[1]
        y_d = _rwkv(pr, rw_mu[l], rw_w0[l], _pad_rows_at(rw_w2[l], 0, LANE), rw_a0[l],
                    _pad_rows_at(rw_a2[l], rank_w, LANE), cast(rw_g2[l]), rw_k_k[l], rw_k_a[l], rw_r_k[l],
                    rw_norm_w[l], rw_norm_b[l], rw_bd, B, S)

        h = _merge(h, g, w_gate, gate_b[l], (y_a, y_b, y_c, y_d), wb, wo, l)
        h = _ffn(h, ffn2_norm[l], f2g, f2u, f2d, final_norm, l, final=(l == depth - 1))
    return h.reshape(B, S, D)
```

```python
import functools
import math

import jax
import jax.numpy as jnp
from jax import lax
from jax.experimental import pallas as pl
from jax.experimental.pallas import tpu as pltpu

F32 = jnp.float32
BF16 = jnp.bfloat16
MXU_DTYPE = BF16

CHUNK = 64
ROPE_THETA = 10000.0
EPS = 1e-6
DA_HEADS, DA_QK, DA_V = 4, 64, 128
GLA_HEADS, GLA_DK, GLA_DV, GLA_RANK, GLA_TAU = 4, 64, 128, 16, 16.0
SSD_HEADS, SSD_P, SSD_GROUPS, SSD_N, SSD_CONV = 8, 64, 2, 128, 4
SSD_INNER = SSD_HEADS * SSD_P
RW_HEADS, RW_HEAD = 8, 64
RW_DIM = RW_HEADS * RW_HEAD
RW_DECAY_SCALE = 0.606531
RW_GN_EPS = 64e-5
LANE = 128
SUBLANE = 8
VMEM_LIMIT = 48 * 1024 * 1024


def _cparams(sem):
    return pltpu.CompilerParams(dimension_semantics=sem, vmem_limit_bytes=VMEM_LIMIT)


def _mm(a, b):
    return jnp.dot(a.astype(MXU_DTYPE), b.astype(MXU_DTYPE), preferred_element_type=F32)


def _mm_nt(a, b):
    return lax.dot_general(a.astype(MXU_DTYPE), b.astype(MXU_DTYPE), (((1,), (1,)), ((), ())),
                           preferred_element_type=F32)


def _mm_tn(a, b):
    return lax.dot_general(a.astype(MXU_DTYPE), b.astype(MXU_DTYPE), (((0,), (0,)), ((), ())),
                           preferred_element_type=F32)


def _split(x):
    hi = x.astype(MXU_DTYPE)
    lo = (x - hi.astype(F32)).astype(MXU_DTYPE)
    return hi, lo


def _sel_l(m01, x):
    hi, lo = _split(x)
    m = m01.astype(MXU_DTYPE)
    return jnp.dot(m, hi, preferred_element_type=F32) + jnp.dot(m, lo, preferred_element_type=F32)


def _sel_r(x, m01):
    hi, lo = _split(x)
    m = m01.astype(MXU_DTYPE)
    return jnp.dot(hi, m, preferred_element_type=F32) + jnp.dot(lo, m, preferred_element_type=F32)


def _sigmoid(x):
    return 1.0 / (1.0 + jnp.exp(-x))


def _silu(x):
    return x * _sigmoid(x)


def _softplus(x):
    return jnp.maximum(x, 0.0) + jnp.log1p(jnp.exp(-jnp.abs(x)))


def _rmsnorm_rows(x, g):
    return x * lax.rsqrt(jnp.mean(x * x, axis=-1, keepdims=True) + EPS) * g


def _tri(n, strict=False):
    r = lax.broadcasted_iota(jnp.int32, (n, n), 0)
    c = lax.broadcasted_iota(jnp.int32, (n, n), 1)
    return (c < r) if strict else (c <= r)


def _ffn_kernel(h_ref, g_ref, wg_ref, wu_ref, wd_ref, fg_ref, o_ref, xn_s, act_s, *, tf, final):
    xn_s[...] = _rmsnorm_rows(h_ref[...], g_ref[...]).astype(xn_s.dtype)
    F = wg_ref.shape[1]
    for f in range(F // tf):
        cols = slice(f * tf, (f + 1) * tf)
        xn = xn_s[...]
        gate = jnp.dot(xn, wg_ref[:, cols], preferred_element_type=F32)
        up = jnp.dot(xn, wu_ref[:, cols], preferred_element_type=F32)
        act_s[:, cols] = (_silu(gate) * up).astype(act_s.dtype)
    out = h_ref[...] + 0.5 * jnp.dot(act_s[...], wd_ref[...], preferred_element_type=F32)
    if final:
        out = _rmsnorm_rows(out, fg_ref[...])
    o_ref[...] = out


def _resident(shape):
    return pl.BlockSpec(shape, lambda *_: (0,) * len(shape), pipeline_mode=pl.Buffered(1))


def _layer_block(w, l):
    return pl.BlockSpec((None,) + w.shape[1:], lambda *_: (l,) + (0,) * (w.ndim - 1),
                        pipeline_mode=pl.Buffered(1))


def _ffn(h, g, wg, wu, wd, fg, l, *, final, tm=512, tf=256):
    T, D = h.shape
    F = wg.shape[2]
    tm = min(tm, T)
    return pl.pallas_call(
        functools.partial(_ffn_kernel, tf=tf, final=final),
        grid=(T // tm,),
        in_specs=[
            pl.BlockSpec((tm, D), lambda i: (i, 0)),
            _resident((1, D)),
            _layer_block(wg, l),
            _layer_block(wu, l),
            _layer_block(wd, l),
            _resident((1, D)),
        ],
        out_specs=pl.BlockSpec((tm, D), lambda i: (i, 0)),
        out_shape=jax.ShapeDtypeStruct((T, D), F32),
        scratch_shapes=[pltpu.VMEM((tm, D), MXU_DTYPE), pltpu.VMEM((tm, F), MXU_DTYPE)],
        compiler_params=_cparams(("parallel",)),
        name="ffn",
    )(h, g.reshape(1, D), wg, wu, wd, fg.reshape(1, D))


def _proj_kernel(h_ref, g_ref, w_ref, o_ref):
    xn = _rmsnorm_rows(h_ref[...], g_ref[...]).astype(w_ref.dtype)
    o_ref[...] = jnp.dot(xn, w_ref[...], preferred_element_type=F32).astype(o_ref.dtype)


def _proj(h, g, w, l, out_dtype, *, tm=512):
    T, D = h.shape
    N = w.shape[2]
    tm = min(tm, T)
    return pl.pallas_call(
        _proj_kernel,
        grid=(T // tm,),
        in_specs=[
            pl.BlockSpec((tm, D), lambda i: (i, 0)),
            pl.BlockSpec((1, D), lambda i: (0, 0)),
            _layer_block(w, l),
        ],
        out_specs=pl.BlockSpec((tm, N), lambda i: (i, 0)),
        out_shape=jax.ShapeDtypeStruct((T, N), out_dtype),
        compiler_params=_cparams(("parallel",)),
        name="proj",
    )(h, g.reshape(1, D), w)


def _proj_da_kernel(h_ref, g_ref, w_ref, cos_ref, sin_ref, o_ref, *, nqk):
    xn = _rmsnorm_rows(h_ref[...], g_ref[...]).astype(w_ref.dtype)
    p = jnp.dot(xn, w_ref[...], preferred_element_type=F32)
    cos = cos_ref[...]
    sin = sin_ref[...]
    tm = p.shape[0]
    lane = lax.broadcasted_iota(jnp.int32, (tm, LANE), 1)
    first_half = (lane % DA_QK) < (DA_QK // 2)
    qscale = DA_QK ** -0.5 * math.log2(math.e)
    for blk in range(2 * nqk):
        t = p[:, blk * LANE:(blk + 1) * LANE]
        swapped = jnp.where(first_half, pltpu.roll(t, LANE - DA_QK // 2, 1), pltpu.roll(t, DA_QK // 2, 1))
        t = t * cos + swapped * sin
        if blk < nqk:
            t = t * qscale
        o_ref[:, blk * LANE:(blk + 1) * LANE] = t.astype(o_ref.dtype)
    ones = jnp.ones((tm, LANE), o_ref.dtype)
    for hd in range(DA_HEADS):
        src = (2 * nqk + hd) * LANE
        dst = (2 * nqk + 2 * hd) * LANE
        o_ref[:, dst:dst + LANE] = p[:, src:src + LANE].astype(o_ref.dtype)
        o_ref[:, dst + LANE:dst + 2 * LANE] = ones


def _proj_da(h, g, w, l, cos_t, sin_t, S, *, tm=512):
    T, D = h.shape
    N = w.shape[2] + DA_HEADS * LANE
    tm = min(tm, S)
    ns = S // tm
    nqk = DA_HEADS * 2 * DA_QK // LANE
    return pl.pallas_call(
        functools.partial(_proj_da_kernel, nqk=nqk),
        grid=(T // tm,),
        in_specs=[
            pl.BlockSpec((tm, D), lambda i: (i, 0)),
            pl.BlockSpec((1, D), lambda i: (0, 0)),
            _layer_block(w, l),
            pl.BlockSpec((tm, LANE), lambda i: (i % ns, 0)),
            pl.BlockSpec((tm, LANE), lambda i: (i % ns, 0)),
        ],
        out_specs=pl.BlockSpec((tm, N), lambda i: (i, 0)),
        out_shape=jax.ShapeDtypeStruct((T, N), MXU_DTYPE),
        compiler_params=_cparams(("parallel",)),
        name="proj_da",
    )(h, g.reshape(1, D), w, cos_t, sin_t)


def _da_kernel(lam_ref, q_ref, k_ref, v_ref, g_ref, o_ref, q_s, m_s, l_s, acc_s,
               *, tq, nsub, hpb, kv_unroll, out_scale):
    qi = pl.program_id(2)
    lane = lax.broadcasted_iota(jnp.int32, (tq, LANE), 1)
    for hd in range(hpb):
        q = q_ref[:, hd * LANE:(hd + 1) * LANE]
        zero = jnp.zeros_like(q)
        q_s[2 * hd] = jnp.where(lane < DA_QK, q, zero)
        q_s[2 * hd + 1] = jnp.where(lane >= DA_QK, q, zero)
    m_s[...] = jnp.full_like(m_s, -jnp.inf)
    l_s[...] = jnp.zeros_like(l_s)
    acc_s[...] = jnp.zeros_like(acc_s)
    tr = tq // nsub
    chains = [(2 * hd + c, hd, pl.ds(r * tr, tr), r) for hd in range(hpb) for r in range(nsub) for c in range(2)]

    def step(j, masked):
        rows = pl.ds(pl.multiple_of(j * tq, tq), tq)
        k = [k_ref[rows, hd * LANE:(hd + 1) * LANE] for hd in range(hpb)]
        v = [v_ref[rows, hd * 2 * LANE:(hd + 1) * 2 * LANE] for hd in range(hpb)]
        nk = [(r + 1) * tr if masked else tq for _, _, _, r in chains]
        s = [lax.dot_general(q_s[n, rs, :], k[hd][:w], (((1,), (1,)), ((), ())), preferred_element_type=F32)
             for (n, hd, rs, _), w in zip(chains, nk)]
        if masked:
            s = [jnp.where(lax.broadcasted_iota(jnp.int32, (tr, w), 1) // CHUNK
                           <= (lax.broadcasted_iota(jnp.int32, (tr, w), 0) + ch[3] * tr) // CHUNK, x, -jnp.inf)
                 for x, ch, w in zip(s, chains, nk)]
        m_prev = [m_s[n, rs, :] for n, _, rs, _ in chains]
        m_new = [jnp.maximum(mp, jnp.max(x, axis=-1, keepdims=True)) for mp, x in zip(m_prev, s)]
        p = [jnp.exp2(x - jnp.concatenate([mn] * (w // LANE), axis=1)) for x, mn, w in zip(s, m_new, nk)]
        alpha = [jnp.exp2(mp - mn) for mp, mn in zip(m_prev, m_new)]
        pv = [jnp.dot(x.astype(MXU_DTYPE), v[ch[1]][:w], preferred_element_type=F32)
              for x, ch, w in zip(p, chains, nk)]
        for (n, _, rs, _), mn, a, o in zip(chains, m_new, alpha, pv):
            m_s[n, rs, :] = mn
            l_s[n, rs, :] = a * l_s[n, rs, :] + o[:, DA_V:]
            acc_s[n, rs, :] = a * acc_s[n, rs, :] + o[:, :DA_V]

    def block_group(i, carry):
        for u in range(kv_unroll):
            step(kv_unroll * i + u, False)
        return carry

    def block_single(j, carry):
        step(j, False)
        return carry

    n_group = qi // kv_unroll
    lax.fori_loop(0, n_group, block_group, 0)
    lax.fori_loop(n_group * kv_unroll, qi, block_single, 0)
    step(qi, True)
    lam = lam_ref[0, 0]
    for hd in range(hpb):
        o = acc_s[2 * hd] / l_s[2 * hd] - lam * (acc_s[2 * hd + 1] / l_s[2 * hd + 1])
        o = _rmsnorm_rows(o, g_ref[...]) * out_scale
        o_ref[:, hd * DA_V:(hd + 1) * DA_V] = o.astype(o_ref.dtype)


def _diff_attention(lam, pda, da_norm, B, S, lam_init, *, tq=512, nsub=2, hpb=2, kv_unroll=4):
    T = pda.shape[0]
    tq = min(tq, S)
    nq = S // tq
    HB = DA_HEADS // hpb
    kernel = functools.partial(_da_kernel, tq=tq, nsub=nsub, hpb=hpb, kv_unroll=kv_unroll,
                               out_scale=1.0 - lam_init)
    return pl.pallas_call(
        kernel,
        grid=(B, HB, nq),
        in_specs=[
            pl.BlockSpec(memory_space=pltpu.SMEM),
            pl.BlockSpec((tq, hpb * LANE), lambda b, h, i: (b * nq + i, h)),
            pl.BlockSpec((S, hpb * LANE), lambda b, h, i: (b, HB + h)),
            pl.BlockSpec((S, hpb * 2 * LANE), lambda b, h, i: (b, HB + h)),
            pl.BlockSpec((1, LANE), lambda b, h, i: (0, 0)),
        ],
        out_specs=pl.BlockSpec((tq, hpb * DA_V), lambda b, h, i: (b * nq + i, h)),
        out_shape=jax.ShapeDtypeStruct((T, DA_HEADS * DA_V), MXU_DTYPE),
        scratch_shapes=[
            pltpu.VMEM((2 * hpb, tq, LANE), MXU_DTYPE),
            pltpu.VMEM((2 * hpb, tq, LANE), F32),
            pltpu.VMEM((2 * hpb, tq, LANE), F32),
            pltpu.VMEM((2 * hpb, tq, DA_V), F32),
        ],
        compiler_params=_cparams(("parallel", "parallel", "arbitrary")),
        name="diff_attn",
    )(lam, pda, pda, pda, da_norm.reshape(1, DA_V))


def _gla_kernel(p_ref, w2_ref, gb_ref, ng_ref, o_ref, st_s, lg_s, *, tc, nb, ahead):
    i = pl.program_id(0)
    HK = GLA_HEADS * GLA_DK
    HV = GLA_HEADS * GLA_DV

    @pl.when(i == 0)
    def _():
        st_s[...] = jnp.zeros_like(st_s)

    for b in range(nb):
        z = _mm(p_ref[b, :, 2 * HK + 2 * HV:], w2_ref[...]) + gb_ref[...]
        lg_s[b] = (jnp.minimum(z, 0.0) - jnp.log1p(jnp.exp(-jnp.abs(z)))) * (1.0 / GLA_TAU)

    tri = _tri(CHUNK).astype(F32)
    causal = _tri(CHUNK)
    lane_k = lax.broadcasted_iota(jnp.int32, (CHUNK, HK), 1) // GLA_DK
    bd = (lax.broadcasted_iota(jnp.int32, (HV, HK), 0) // GLA_DV
          == lax.broadcasted_iota(jnp.int32, (HV, HK), 1) // GLA_DK)
    ng = ng_ref[...]

    def body(c, carry):
        inst = [(b, pl.ds(pl.multiple_of((c * ahead + u) * CHUNK, CHUNK), CHUNK))
                for u in range(ahead) for b in range(nb)]
        q = [p_ref[b, rows, 0:HK] * (GLA_DK ** -0.5) for b, rows in inst]
        k = [p_ref[b, rows, HK:2 * HK] for b, rows in inst]
        v = [p_ref[b, rows, 2 * HK:2 * HK + HV].astype(MXU_DTYPE) for b, rows in inst]
        G = [_sel_l(tri, lg_s[b, rows, :]) for b, rows in inst]
        g_mid = [x[CHUNK // 2:CHUNK // 2 + 1] for x in G]
        g_last = [x[CHUNK - 1:CHUNK] for x in G]
        qe = [x * jnp.exp(g - gm) for x, g, gm in zip(q, G, g_mid)]
        ke = [(x * jnp.exp(gm - g)).astype(MXU_DTYPE) for x, g, gm in zip(k, G, g_mid)]
        q_in = [x * jnp.exp(g) for x, g in zip(q, G)]
        k_end = [x * jnp.exp(gl - g) for x, g, gl in zip(k, G, g_last)]
        att = [[jnp.where(causal, _mm_nt(jnp.where(lane_k == h, x, 0.0), y), 0.0) for x, y in zip(qe, ke)]
               for h in range(GLA_HEADS)]
        y_intra = [[_mm(a, x[:, h * GLA_DV:(h + 1) * GLA_DV]) for a, x in zip(att[h], v)]
                   for h in range(GLA_HEADS)]
        kv = [jnp.where(bd, _mm_tn(x, y), 0.0) for x, y in zip(v, k_end)]
        for n, (b, rows) in enumerate(inst):
            st = st_s[b]
            y_inter = _mm_nt(q_in[n], st)
            st_s[b] = st * jnp.exp(g_last[n]) + kv[n]
            outs = [_rmsnorm_rows(y_intra[h][n] + y_inter[:, h * GLA_DV:(h + 1) * GLA_DV], ng)
                    for h in range(GLA_HEADS)]
            og = p_ref[b, rows, 2 * HK + HV:2 * HK + 2 * HV]
            o_ref[b, rows, :] = (jnp.concatenate(outs, axis=1) * _silu(og)).astype(o_ref.dtype)
        return carry

    lax.fori_loop(0, tc // (CHUNK * ahead), body, 0)


def _gla(pg, w2pad, gate_b, gla_norm, B, S, *, tc=512, ahead=4):
    T, N = pg.shape
    tc = min(tc, S)
    ns = S // tc
    HK, HV = GLA_HEADS * GLA_DK, GLA_HEADS * GLA_DV
    const = lambda i: (0, 0)
    out = pl.pallas_call(
        functools.partial(_gla_kernel, tc=tc, nb=B, ahead=ahead),
        grid=(ns,),
        in_specs=[
            pl.BlockSpec((B, tc, N), lambda i: (0, i, 0)),
            pl.BlockSpec((LANE, HK), const),
            pl.BlockSpec((1, HK), const),
            pl.BlockSpec((1, GLA_DV), const),
        ],
        out_specs=pl.BlockSpec((B, tc, HV), lambda i: (0, i, 0)),
        out_shape=jax.ShapeDtypeStruct((B, S, HV), MXU_DTYPE),
        scratch_shapes=[pltpu.VMEM((B, HV, HK), F32), pltpu.VMEM((B, tc, HK), F32)],
        compiler_params=_cparams(("arbitrary",)),
        name="gla",
    )(pg.reshape(B, S, N), w2pad, gate_b.reshape(1, HK), gla_norm.reshape(1, GLA_DV))
    return out.reshape(T, HV)


def _ssd_kernel(p_ref, cw_ref, cb_ref, dtb_ref, a_ref, e64_ref, e128_ref, dx_ref, ng_ref, o_ref,
                st_s, xpad_s, xs_s, bc_s, ax_s, ax2_s, xdt_s, *, tc, nb, ahead):
    i = pl.program_id(0)
    NI = SSD_INNER
    GN = SSD_GROUPS * SSD_N
    CD = NI + 2 * GN
    PAD = SUBLANE

    @pl.when(i == 0)
    def _():
        st_s[...] = jnp.zeros_like(st_s)
        xpad_s[:, 0:PAD, :] = jnp.zeros((nb, PAD, CD), F32)

    for b in range(nb):
        xpad_s[b, PAD:PAD + tc, :] = p_ref[b, :, NI:NI + CD]
        conv = cb_ref[...]
        for kk in range(SSD_CONV):
            off = PAD - (SSD_CONV - 1) + kk
            conv = conv + cw_ref[kk:kk + 1, :] * xpad_s[b, off:off + tc, :]
        xpad_s[b, 0:PAD, :] = xpad_s[b, tc:tc + PAD, :]
        xbc = _silu(conv)
        xs_s[b] = xbc[:, 0:NI]
        bc_s[b] = xbc[:, NI:]
        dt = _softplus(p_ref[b, :, NI + CD:] + dtb_ref[...])
        dta = dt * a_ref[...]
        ax_s[b] = _sel_r(dta, e64_ref[...])
        ax2_s[b] = _sel_r(dta, e128_ref[...])
        xdt_s[b] = xbc[:, 0:NI] * _sel_r(dt, e64_ref[...])

    tri = _tri(CHUNK).astype(F32)
    tri_t = (lax.broadcasted_iota(jnp.int32, (CHUNK, CHUNK), 0)
             <= lax.broadcasted_iota(jnp.int32, (CHUNK, CHUNK), 1)).astype(F32)
    causal = _tri(CHUNK)
    low_half = lax.broadcasted_iota(jnp.int32, (CHUNK, LANE), 1) < SSD_P
    rep = SSD_HEADS // SSD_GROUPS
    GW = rep * SSD_P

    def body(c, carry):
        inst = [(b, pl.ds(pl.multiple_of((c * ahead + u) * CHUNK, CHUNK), CHUNK))
                for u in range(ahead) for b in range(nb)]
        xdt = [xdt_s[b, rows, :] for b, rows in inst]
        acx = [_sel_l(tri, ax_s[b, rows, :]) for b, rows in inst]
        acx2 = [_sel_l(tri, ax2_s[b, rows, :]) for b, rows in inst]
        dta2 = [ax2_s[b, rows, :] for b, rows in inst]
        a_last = [x[CHUNK - 1:CHUNK] for x in acx]
        x_end = [(x * jnp.exp(al - a)).astype(MXU_DTYPE) for x, a, al in zip(xdt, acx, a_last)]
        xdt_m = [x.astype(MXU_DTYPE) for x in xdt]
        bm = [[bc_s[b, rows, g * SSD_N:(g + 1) * SSD_N].astype(MXU_DTYPE) for b, rows in inst]
              for g in range(SSD_GROUPS)]
        cm = [[bc_s[b, rows, GN + g * SSD_N:GN + (g + 1) * SSD_N].astype(MXU_DTYPE) for b, rows in inst]
              for g in range(SSD_GROUPS)]
        cb = [[_mm_nt(x, y) for x, y in zip(cm[g], bm[g])] for g in range(SSD_GROUPS)]
        new_st = [jnp.concatenate([_mm_tn(bm[g][n], x_end[n][:, g * GW:(g + 1) * GW])
                                   for g in range(SSD_GROUPS)], axis=1) for n in range(len(inst))]
        y_intra = []
        for n in range(len(inst)):
            parts = []
            for pair in range(SSD_HEADS // 2):
                res = []
                for hh in range(2):
                    h = pair * 2 + hh
                    col = acx2[n][:, h * LANE:h * LANE + CHUNK]
                    row = jnp.sum(tri_t * dta2[n][:, h * LANE:h * LANE + CHUNK], axis=0, keepdims=True)
                    lmat = jnp.exp(jnp.where(causal, col - row, -jnp.inf))
                    res.append(_mm(cb[h // rep][n] * lmat, xdt_m[n][:, pair * LANE:(pair + 1) * LANE]))
                parts.append(jnp.where(low_half, res[0], res[1]))
            y_intra.append(jnp.concatenate(parts, axis=1))
        half = NI // SSD_GROUPS
        for n, (b, rows) in enumerate(inst):
            st = st_s[b]
            inter = jnp.concatenate([_mm(cm[g][n], st[:, g * GW:(g + 1) * GW]) for g in range(SSD_GROUPS)],
                                    axis=1)
            st_s[b] = st * jnp.exp(a_last[n]) + new_st[n]
            y = y_intra[n] + inter * jnp.exp(acx[n]) + dx_ref[...] * xs_s[b, rows, :]
            y = y * _silu(p_ref[b, rows, 0:NI])
            outs = [_rmsnorm_rows(y[:, g * half:(g + 1) * half], ng_ref[:, g * half:(g + 1) * half])
                    for g in range(SSD_GROUPS)]
            o_ref[b, rows, :] = jnp.concatenate(outs, axis=1).astype(o_ref.dtype)
        return carry

    lax.fori_loop(0, tc // (CHUNK * ahead), body, 0)


def _ssd(ps, conv_w, conv_b, dtb_pad, a_pad, e64, e128, d_x, ssd_norm, B, S, *, tc=512, ahead=2):
    T, N = ps.shape
    tc = min(tc, S)
    ns = S // tc
    NI = SSD_INNER
    CD = NI + 2 * SSD_GROUPS * SSD_N
    const = lambda i: (0, 0)
    out = pl.pallas_call(
        functools.partial(_ssd_kernel, tc=tc, nb=B, ahead=ahead),
        grid=(ns,),
        in_specs=[
            pl.BlockSpec((B, tc, N), lambda i: (0, i, 0)),
            pl.BlockSpec((SSD_CONV, CD), const),
            pl.BlockSpec((1, CD), const),
            pl.BlockSpec((1, LANE), const),
            pl.BlockSpec((1, LANE), const),
            pl.BlockSpec((LANE, NI), const),
            pl.BlockSpec((LANE, SSD_HEADS * LANE), const),
            pl.BlockSpec((1, NI), const),
            pl.BlockSpec((1, NI), const),
        ],
        out_specs=pl.BlockSpec((B, tc, NI), lambda i: (0, i, 0)),
        out_shape=jax.ShapeDtypeStruct((B, S, NI), MXU_DTYPE),
        scratch_shapes=[
            pltpu.VMEM((B, SSD_N, NI), F32),
            pltpu.VMEM((B, tc + SUBLANE, CD), F32),
            pltpu.VMEM((B, tc, NI), F32),
            pltpu.VMEM((B, tc, 2 * SSD_GROUPS * SSD_N), F32),
            pltpu.VMEM((B, tc, NI), F32),
            pltpu.VMEM((B, tc, SSD_HEADS * LANE), F32),
            pltpu.VMEM((B, tc, NI), F32),
        ],
        compiler_params=_cparams(("arbitrary",)),
        name="ssd",
    )(ps.reshape(B, S, N), conv_w, conv_b.reshape(1, CD), dtb_pad, a_pad, e64, e128, d_x,
      ssd_norm.reshape(1, NI))
    return out.reshape(T, NI)


def _unit_lower_inverse(mats, eye, same_blk, stack):
    d = [jnp.where(same_blk, a, 0.0) for a in mats]
    o = [a - x for a, x in zip(mats, d)]
    t = [eye + x for x in d]
    dp = d
    sd = [stack(x) for x in dp]
    for _ in range(3):
        dp = [_mm(x, y) for x, y in zip(dp, sd)]
        sd = [stack(x) for x in dp]
        t = [x + _mm(x, y) for x, y in zip(t, sd)]
    n = [_mm(x, stack(y)) for x, y in zip(t, o)]
    n2 = [_mm(x, stack(x)) for x in n]
    q = [eye + x for x in n]
    q = [x + _mm(x, stack(y)) for x, y in zip(q, n2)]
    return [_mm(x, stack(y)) for x, y in zip(q, t)]


def _rw_kernel(p_ref, mu_ref, w0_ref, w2_ref, a0_ref, a2_ref, g2_ref, kk_ref, ka_ref, rk_ref,
               nw_ref, nb_ref, bd_ref, o_ref,
               mt_s, ppad_s, r_s, k_s, v_s, lw_s, al_s, be_s, g_s, *, tc, nb, ahead):
    i = pl.program_id(0)
    R = RW_DIM
    PAD = SUBLANE
    NP = p_ref.shape[2]
    NPAIR = RW_HEADS // 2

    @pl.when(i == 0)
    def _():
        mt_s[...] = jnp.zeros_like(mt_s)
        ppad_s[:, 0:PAD, :] = jnp.zeros((nb, PAD, NP), F32)

    bd = bd_ref[...]
    for b in range(nb):
        p = p_ref[b]
        ppad_s[b, PAD:PAD + tc, :] = p
        prev = ppad_s[b, PAD - 1:PAD - 1 + tc, :]
        ppad_s[b, 0:PAD, :] = ppad_s[b, tc:tc + PAD, :]
        pm = p + (prev - p) * mu_ref[...]
        r = pm[:, 0:R]
        k = pm[:, R:2 * R]
        v = pm[:, 2 * R:3 * R]
        wa = pm[:, 3 * R:3 * R + LANE]
        gl = pm[:, 3 * R + LANE:3 * R + 2 * LANE]
        lw_s[b] = -RW_DECAY_SCALE * _sigmoid(w0_ref[...] + _mm(jnp.tanh(wa), w2_ref[...]))
        a = _sigmoid(a0_ref[...] + _mm(wa, a2_ref[...]))
        g_s[b] = _mm(_sigmoid(gl), g2_ref[...])
        kk = k * kk_ref[...]
        kk = kk / jnp.maximum(jnp.sqrt(_sel_r(kk * kk, bd)), 1e-12)
        r_s[b] = r
        k_s[b] = k * (1.0 + (a - 1.0) * ka_ref[...])
        v_s[b] = v
        al_s[b] = -kk
        be_s[b] = kk * a

    C2 = 2 * CHUNK
    tri = _tri(CHUNK).astype(F32)
    tt = lax.broadcasted_iota(jnp.int32, (CHUNK, C2), 0)
    ss = lax.broadcasted_iota(jnp.int32, (CHUNK, C2), 1) % CHUNK
    strict = ss < tt
    incl = ss <= tt
    eye = (ss == tt).astype(F32)
    same_blk = (tt // 16) == (ss // 16)
    m0 = lax.broadcasted_iota(jnp.int32, (CHUNK, LANE), 1) < RW_HEAD
    bdm = (lax.broadcasted_iota(jnp.int32, (C2, C2), 0) // CHUNK
           == lax.broadcasted_iota(jnp.int32, (C2, C2), 1) // CHUNK)

    def stack(x):
        xb = x.astype(MXU_DTYPE)
        zero = jnp.zeros_like(xb)
        return jnp.concatenate([jnp.where(m0, xb, zero), jnp.where(m0, zero, xb)], axis=0)

    def body(c, carry):
        row_sl = [pl.ds(pl.multiple_of((c * ahead + u) * CHUNK, CHUNK), CHUNK) for u in range(ahead)]
        w_end, vv_b, wide = [], [], []
        for rows in row_sl:
            for b in range(nb):
                lw = lw_s[b, rows, :]
                cum = _sel_l(tri, lw)
                e_cum = jnp.exp(cum)
                e_inv = jnp.exp(-cum)
                ah = al_s[b, rows, :] * jnp.exp(cum - lw)
                bh = be_s[b, rows, :] * e_inv
                kh = k_s[b, rows, :] * e_inv
                rh = r_s[b, rows, :] * e_cum
                vv = v_s[b, rows, :]
                vv_b.append(vv)
                for j in range(NPAIR):
                    sl = slice(j * LANE, (j + 1) * LANE)
                    w_end.append(e_cum[CHUNK - 1:CHUNK, sl])
                    wide.append(tuple(x[:, sl].astype(MXU_DTYPE) for x in (ah, rh, bh, kh, vv)))
        nch = nb * NPAIR
        ar = [jnp.concatenate([w[0], w[1]], axis=0) for w in wide]
        bk = [jnp.concatenate([stack(w[2]), stack(w[3])], axis=0) for w in wide]
        v_bd = [stack(w[4]) for w in wide]
        big = [_mm_nt(x, y) for x, y in zip(ar, bk)]
        a_ab = [jnp.where(strict, x[0:CHUNK, 0:C2], 0.0) for x in big]
        a_ak = [jnp.where(strict, x[0:CHUNK, C2:], 0.0) for x in big]
        r_b = [jnp.where(incl, x[CHUNK:, 0:C2], 0.0) for x in big]
        r_k = [jnp.where(incl, x[CHUNK:, C2:], 0.0) for x in big]
        t_inv = _unit_lower_inverse(a_ab, eye, same_blk, stack)
        for u, rows in enumerate(row_sl):
            ch = slice(u * nch, (u + 1) * nch)
            mt = [mt_s[n] for n in range(nch)]
            rhs = [_mm_nt(w[0], m) + _mm(x, v) for w, m, x, v in zip(wide[ch], mt, a_ak[ch], v_bd[ch])]
            u_w = [_mm(x, stack(y)) for x, y in zip(t_inv[ch], rhs)]
            upd = [_mm_tn(jnp.concatenate([x.astype(MXU_DTYPE), w[4]], axis=0),
                          jnp.concatenate([w[2], w[3]], axis=0)) for x, w in zip(u_w, wide[ch])]
            for n in range(nch):
                mt_s[n] = (mt[n] + jnp.where(bdm, upd[n], 0.0)) * w_end[u * nch + n]
            y_w = [_mm_nt(w[1], m) + _mm(x, stack(uu)) + _mm(z, v)
                   for w, m, x, uu, z, v in zip(wide[ch], mt, r_b[ch], u_w, r_k[ch], v_bd[ch])]
            for b in range(nb):
                y = jnp.concatenate([y_w[b * NPAIR + j] for j in range(NPAIR)], axis=1)
                mean = _sel_r(y, bd) * (1.0 / RW_HEAD)
                yc = y - mean
                var = _sel_r(yc * yc, bd) * (1.0 / RW_HEAD)
                yn = yc * lax.rsqrt(var + RW_GN_EPS) * nw_ref[...] + nb_ref[...]
                bonus = _sel_r(r_s[b, rows, :] * k_s[b, rows, :] * rk_ref[...], bd) * vv_b[u * nb + b]
                o_ref[b, rows, :] = ((yn + bonus) * g_s[b, rows, :]).astype(o_ref.dtype)
        return carry

    lax.fori_loop(0, tc // (CHUNK * ahead), body, 0)


def _rwkv(pr, mu, w0, w2pad, a0, a2pad, g2, k_k, k_a, r_k, norm_w, norm_b, bd, B, S, *, tc=256, ahead=4):
    T, N = pr.shape
    tc = min(tc, S)
    ns = S // tc
    R = RW_DIM
    const = lambda i: (0, 0)
    row = lambda x: x.reshape(1, -1)
    vec = pl.BlockSpec((1, R), const)
    out = pl.pallas_call(
        functools.partial(_rw_kernel, tc=tc, nb=B, ahead=ahead),
        grid=(ns,),
        in_specs=[
            pl.BlockSpec((B, tc, N), lambda i: (0, i, 0)),
            pl.BlockSpec((1, N), const),
            vec,
            pl.BlockSpec((LANE, R), const),
            vec,
            pl.BlockSpec((LANE, R), const),
            pl.BlockSpec((LANE, R), const),
            vec, vec, vec, vec, vec,
            pl.BlockSpec((R, R), const),
        ],
        out_specs=pl.BlockSpec((B, tc, R), lambda i: (0, i, 0)),
        out_shape=jax.ShapeDtypeStruct((B, S, R), MXU_DTYPE),
        scratch_shapes=[
            pltpu.VMEM((B * RW_HEADS // 2, LANE, LANE), F32),
            pltpu.VMEM((B, tc + SUBLANE, N), F32),
        ] + [pltpu.VMEM((B, tc, R), F32) for _ in range(7)],
        compiler_params=_cparams(("arbitrary",)),
        name="rwkv7",
    )(pr.reshape(B, S, N), row(mu), row(w0), w2pad, row(a0), a2pad, g2, row(k_k), row(k_a), row(r_k),
      row(norm_w), row(norm_b), bd)
    return out.reshape(T, R)


def _merge_kernel(h_ref, g_ref, wgate_ref, gb_ref, ya_ref, yb_ref, yc_ref, yd_ref, wb_ref, wo_ref, o_ref):
    h = h_ref[...]
    D = h.shape[1]
    xn = _rmsnorm_rows(h, g_ref[...]).astype(wgate_ref.dtype)
    merged = jnp.zeros(h.shape, F32)
    for n, y_ref in enumerate((ya_ref, yb_ref, yc_ref, yd_ref)):
        logits = jnp.dot(xn, wgate_ref[:, n * D:(n + 1) * D], preferred_element_type=F32) + gb_ref[n:n + 1, :]
        proj = jnp.dot(y_ref[...], wb_ref[n], preferred_element_type=F32)
        merged = merged + _sigmoid(logits) * proj
    o_ref[...] = h + jnp.dot(merged.astype(wo_ref.dtype), wo_ref[...], preferred_element_type=F32)


def _merge(h, g, wgate, gate_b, ys, w_branch, w_out, l, *, tm=512):
    T, D = h.shape
    tm = min(tm, T)
    _, NB, MW, _ = w_branch.shape
    const2 = lambda i: (0, 0)
    ytile = pl.BlockSpec((tm, MW), lambda i: (i, 0))
    return pl.pallas_call(
        _merge_kernel,
        grid=(T // tm,),
        in_specs=[
            pl.BlockSpec((tm, D), lambda i: (i, 0)),
            pl.BlockSpec((1, D), const2),
            _layer_block(wgate, l),
            pl.BlockSpec((NB, D), const2),
            ytile, ytile, ytile, ytile,
            _layer_block(w_branch, l),
            _layer_block(w_out, l),
        ],
        out_specs=pl.BlockSpec((tm, D), lambda i: (i, 0)),
        out_shape=jax.ShapeDtypeStruct((T, D), F32),
        compiler_params=_cparams(("parallel",)),
        name="merge",
    )(h, g.reshape(1, D), wgate, gate_b, *ys, w_branch, w_out)


def _pad_cols(w, n):
    return jnp.pad(w, ((0, 0), (0, n - w.shape[1])))


def _pad_rows_at(w, start, total):
    return jnp.pad(w, ((start, total - start - w.shape[0]), (0, 0)))


def _rope_tables(S):
    half = DA_QK // 2
    inv_freq = ROPE_THETA ** (-jnp.arange(half, dtype=F32) / half)
    ang = jnp.arange(S, dtype=F32)[:, None] * inv_freq[None, :]
    cos, sin = jnp.cos(ang), jnp.sin(ang)
    reps = LANE // DA_QK
    return jnp.tile(jnp.concatenate([cos, cos], axis=1), (1, reps)), jnp.tile(jnp.concatenate([-sin, sin], axis=1), (1, reps))


def _head_expand(heads, width):
    r = jnp.arange(LANE)[:, None]
    c = jnp.arange(heads * width)[None, :] // width
    return (r == c).astype(F32)


def kernel(x, ffn1_norm, ffn1_wg, ffn1_wu, ffn1_wd, mix_norm, w_in, da_lambda_q1, da_lambda_k1, da_lambda_q2, da_lambda_k2, da_norm, gla_gate_w2, gla_gate_b, gla_norm, ssd_conv_w, ssd_conv_b, ssd_dt_bias, ssd_a_log, ssd_d, ssd_norm, rw_mu, rw_w0, rw_w2, rw_a0, rw_a2, rw_g2, rw_k_k, rw_k_a, rw_r_k, rw_norm_w, rw_norm_b, w_branch, gate_b, w_out, ffn2_norm, ffn2_wg, ffn2_wu, ffn2_wd, final_norm):
    B, S, D = x.shape
    depth = w_in.shape[0]
    T = B * S
    h = x.reshape(T, D)
    cos_t, sin_t = _rope_tables(S)
    e64 = _head_expand(SSD_HEADS, SSD_P)
    e128 = _head_expand(SSD_HEADS, LANE)
    rw_bd = (jnp.arange(RW_DIM)[:, None] // RW_HEAD == jnp.arange(RW_DIM)[None, :] // RW_HEAD).astype(F32)

    n_da = DA_HEADS * (4 * DA_QK + DA_V)
    n_gla = 2 * GLA_HEADS * GLA_DK + 2 * GLA_HEADS * GLA_DV + GLA_RANK
    n_ssd = SSD_INNER + (SSD_INNER + 2 * SSD_GROUPS * SSD_N) + SSD_HEADS
    n_rw = 3 * RW_DIM + rw_w2.shape[1] + rw_a2.shape[1] + rw_g2.shape[1]
    o_gla, o_ssd, o_rw, o_gate = n_da, n_da + n_gla, n_da + n_gla + n_ssd, n_da + n_gla + n_ssd + n_rw
    cast = lambda w: w.astype(MXU_DTYPE)
    pad_last = lambda w: jnp.pad(w, ((0, 0), (0, 0), (0, -w.shape[2] % LANE)))
    f1g, f1u, f1d = cast(ffn1_wg), cast(ffn1_wu), cast(ffn1_wd)
    f2g, f2u, f2d = cast(ffn2_wg), cast(ffn2_wu), cast(ffn2_wd)
    w_da = cast(w_in[:, :, 0:n_da])
    w_gla = cast(pad_last(w_in[:, :, o_gla:o_ssd]))
    w_ssd = cast(pad_last(w_in[:, :, o_ssd:o_rw]))
    w_rw = cast(w_in[:, :, o_rw:o_gate])
    w_gate = cast(w_in[:, :, o_gate:])
    wb, wo = cast(w_branch), cast(w_out)

    for l in range(depth):
        h = _ffn(h, ffn1_norm[l], f1g, f1u, f1d, final_norm, l, final=False)
        g = mix_norm[l]

        lam_init = 0.8 - 0.6 * math.exp(-0.3 * l)
        lam = (jnp.exp(jnp.sum(da_lambda_q1[l] * da_lambda_k1[l])) - jnp.exp(jnp.sum(da_lambda_q2[l] * da_lambda_k2[l]))
               + lam_init).reshape(1, 1).astype(F32)
        pda = _proj_da(h, g, w_da, l, cos_t, sin_t, S)
        y_a = _diff_attention(lam, pda, da_norm[l], B, S, lam_init)

        pg = _proj(h, g, w_gla, l, F32)
        y_b = _gla(pg, _pad_rows_at(gla_gate_w2[l], 0, LANE), gla_gate_b[l], gla_norm[l], B, S)

        ps = _proj(h, g, w_ssd, l, F32)
        dtb_pad = _pad_cols(ssd_dt_bias[l].reshape(1, -1), LANE)
        a_pad = _pad_cols(-jnp.exp(ssd_a_log[l]).reshape(1, -1), LANE)
        d_x = jnp.repeat(ssd_d[l], SSD_P).reshape(1, -1)
        y_c = _ssd(ps, ssd_conv_w[l], ssd_conv_b[l], dtb_pad, a_pad, e64, e128, d_x, ssd_norm[l], B, S)

        pr = _proj(h, g, w_rw, l, F32)
        rank_w = rw_w2.shape[1]
        y_d = _rwkv(pr, rw_mu[l], rw_w0[l], _pad_rows_at(rw_w2[l], 0, LANE), rw_a0[l],
                    _pad_rows_at(rw_a2[l], rank_w, LANE), cast(rw_g2[l]), rw_k_k[l], rw_k_a[l], rw_r_k[l],
                    rw_norm_w[l], rw_norm_b[l], rw_bd, B, S)

        h = _merge(h, g, w_gate, gate_b[l], (y_a, y_b, y_c, y_d), wb, wo, l)
        h = _ffn(h, ffn2_norm[l], f2g, f2u, f2d, final_norm, l, final=(l == depth - 1))
    return h.reshape(B, S, D)
```

```python
import functools
import math

import jax
import jax.numpy as jnp
from jax import lax
from jax.experimental import pallas as pl
from jax.experimental.pallas import tpu as pltpu

F32 = jnp.float32
BF16 = jnp.bfloat16
MXU_DTYPE = BF16

CHUNK = 64
ROPE_THETA = 10000.0
EPS = 1e-6
DA_HEADS, DA_QK, DA_V = 4, 64, 128
GLA_HEADS, GLA_DK, GLA_DV, GLA_RANK, GLA_TAU = 4, 64, 128, 16, 16.0
SSD_HEADS, SSD_P, SSD_GROUPS, SSD_N, SSD_CONV = 8, 64, 2, 128, 4
SSD_INNER = SSD_HEADS * SSD_P
RW_HEADS, RW_HEAD = 8, 64
RW_DIM = RW_HEADS * RW_HEAD
RW_DECAY_SCALE = 0.606531
RW_GN_EPS = 64e-5
LANE = 128
SUBLANE = 8
VMEM_LIMIT = 48 * 1024 * 1024


def _cparams(sem):
    return pltpu.CompilerParams(dimension_semantics=sem, vmem_limit_bytes=VMEM_LIMIT)


def _mm(a, b):
    return jnp.dot(a.astype(MXU_DTYPE), b.astype(MXU_DTYPE), preferred_element_type=F32)


def _mm_nt(a, b):
    return lax.dot_general(a.astype(MXU_DTYPE), b.astype(MXU_DTYPE), (((1,), (1,)), ((), ())),
                           preferred_element_type=F32)


def _mm_tn(a, b):
    return lax.dot_general(a.astype(MXU_DTYPE), b.astype(MXU_DTYPE), (((0,), (0,)), ((), ())),
                           preferred_element_type=F32)


def _split(x):
    hi = x.astype(MXU_DTYPE)
    lo = (x - hi.astype(F32)).astype(MXU_DTYPE)
    return hi, lo


def _sel_l(m01, x):
    hi, lo = _split(x)
    m = m01.astype(MXU_DTYPE)
    return jnp.dot(m, hi, preferred_element_type=F32) + jnp.dot(m, lo, preferred_element_type=F32)


def _sel_r(x, m01):
    hi, lo = _split(x)
    m = m01.astype(MXU_DTYPE)
    return jnp.dot(hi, m, preferred_element_type=F32) + jnp.dot(lo, m, preferred_element_type=F32)


def _sigmoid(x):
    return 1.0 / (1.0 + jnp.exp(-x))


def _silu(x):
    return x * _sigmoid(x)


def _softplus(x):
    return jnp.maximum(x, 0.0) + jnp.log1p(jnp.exp(-jnp.abs(x)))


def _rmsnorm_rows(x, g):
    return x * lax.rsqrt(jnp.mean(x * x, axis=-1, keepdims=True) + EPS) * g


def _tri(n, strict=False):
    r = lax.broadcasted_iota(jnp.int32, (n, n), 0)
    c = lax.broadcasted_iota(jnp.int32, (n, n), 1)
    return (c < r) if strict else (c <= r)


def _ffn_kernel(h_ref, g_ref, wg_ref, wu_ref, wd_ref, fg_ref, o_ref, xn_s, act_s, *, tf, final):
    xn_s[...] = _rmsnorm_rows(h_ref[...], g_ref[...]).astype(xn_s.dtype)
    F = wg_ref.shape[1]
    for f in range(F // tf):
        cols = slice(f * tf, (f + 1) * tf)
        xn = xn_s[...]
        gate = jnp.dot(xn, wg_ref[:, cols], preferred_element_type=F32)
        up = jnp.dot(xn, wu_ref[:, cols], preferred_element_type=F32)
        act_s[:, cols] = (_silu(gate) * up).astype(act_s.dtype)
    out = h_ref[...] + 0.5 * jnp.dot(act_s[...], wd_ref[...], preferred_element_type=F32)
    if final:
        out = _rmsnorm_rows(out, fg_ref[...])
    o_ref[...] = out


def _resident(shape):
    return pl.BlockSpec(shape, lambda *_: (0,) * len(shape), pipeline_mode=pl.Buffered(1))


def _layer_block(w, l):
    return pl.BlockSpec((None,) + w.shape[1:], lambda *_: (l,) + (0,) * (w.ndim - 1),
                        pipeline_mode=pl.Buffered(1))


def _ffn(h, g, wg, wu, wd, fg, l, *, final, tm=512, tf=256):
    T, D = h.shape
    F = wg.shape[2]
    tm = min(tm, T)
    return pl.pallas_call(
        functools.partial(_ffn_kernel, tf=tf, final=final),
        grid=(T // tm,),
        in_specs=[
            pl.BlockSpec((tm, D), lambda i: (i, 0)),
            _resident((1, D)),
            _layer_block(wg, l),
            _layer_block(wu, l),
            _layer_block(wd, l),
            _resident((1, D)),
        ],
        out_specs=pl.BlockSpec((tm, D), lambda i: (i, 0)),
        out_shape=jax.ShapeDtypeStruct((T, D), F32),
        scratch_shapes=[pltpu.VMEM((tm, D), MXU_DTYPE), pltpu.VMEM((tm, F), MXU_DTYPE)],
        compiler_params=_cparams(("parallel",)),
        name="ffn",
    )(h, g.reshape(1, D), wg, wu, wd, fg.reshape(1, D))


def _proj_kernel(h_ref, g_ref, w_ref, o_ref):
    xn = _rmsnorm_rows(h_ref[...], g_ref[...]).astype(w_ref.dtype)
    o_ref[...] = jnp.dot(xn, w_ref[...], preferred_element_type=F32).astype(o_ref.dtype)


def _proj(h, g, w, l, out_dtype, *, tm=512):
    T, D = h.shape
    N = w.shape[2]
    tm = min(tm, T)
    return pl.pallas_call(
        _proj_kernel,
        grid=(T // tm,),
        in_specs=[
            pl.BlockSpec((tm, D), lambda i: (i, 0)),
            pl.BlockSpec((1, D), lambda i: (0, 0)),
            _layer_block(w, l),
        ],
        out_specs=pl.BlockSpec((tm, N), lambda i: (i, 0)),
        out_shape=jax.ShapeDtypeStruct((T, N), out_dtype),
        compiler_params=_cparams(("parallel",)),
        name="proj",
    )(h, g.reshape(1, D), w)


def _proj_da_kernel(h_ref, g_ref, w_ref, cos_ref, sin_ref, o_ref, *, nqk):
    xn = _rmsnorm_rows(h_ref[...], g_ref[...]).astype(w_ref.dtype)
    p = jnp.dot(xn, w_ref[...], preferred_element_type=F32)
    cos = cos_ref[...]
    sin = sin_ref[...]
    tm = p.shape[0]
    lane = lax.broadcasted_iota(jnp.int32, (tm, LANE), 1)
    first_half = (lane % DA_QK) < (DA_QK // 2)
    qscale = DA_QK ** -0.5 * math.log2(math.e)
    for blk in range(2 * nqk):
        t = p[:, blk * LANE:(blk + 1) * LANE]
        swapped = jnp.where(first_half, pltpu.roll(t, LANE - DA_QK // 2, 1), pltpu.roll(t, DA_QK // 2, 1))
        t = t * cos + swapped * sin
        if blk < nqk:
            t = t * qscale
        o_ref[:, blk * LANE:(blk + 1) * LANE] = t.astype(o_ref.dtype)
    ones = jnp.ones((tm, LANE), o_ref.dtype)
    for hd in range(DA_HEADS):
        src = (2 * nqk + hd) * LANE
        dst = (2 * nqk + 2 * hd) * LANE
        o_ref[:, dst:dst + LANE] = p[:, src:src + LANE].astype(o_ref.dtype)
        o_ref[:, dst + LANE:dst + 2 * LANE] = ones


def _proj_da(h, g, w, l, cos_t, sin_t, S, *, tm=512):
    T, D = h.shape
    N = w.shape[2] + DA_HEADS * LANE
    tm = min(tm, S)
    ns = S // tm
    nqk = DA_HEADS * 2 * DA_QK // LANE
    return pl.pallas_call(
        functools.partial(_proj_da_kernel, nqk=nqk),
        grid=(T // tm,),
        in_specs=[
            pl.BlockSpec((tm, D), lambda i: (i, 0)),
            pl.BlockSpec((1, D), lambda i: (0, 0)),
            _layer_block(w, l),
            pl.BlockSpec((tm, LANE), lambda i: (i % ns, 0)),
            pl.BlockSpec((tm, LANE), lambda i: (i % ns, 0)),
        ],
        out_specs=pl.BlockSpec((tm, N), lambda i: (i, 0)),
        out_shape=jax.ShapeDtypeStruct((T, N), MXU_DTYPE),
        compiler_params=_cparams(("parallel",)),
        name="proj_da",
    )(h, g.reshape(1, D), w, cos_t, sin_t)


def _da_kernel(lam_ref, q_ref, k_ref, v_ref, g_ref, o_ref, q_s, m_s, l_s, acc_s,
               *, tq, nsub, hpb, kv_unroll, out_scale):
    qi = pl.program_id(2)
    lane = lax.broadcasted_iota(jnp.int32, (tq, LANE), 1)
    for hd in range(hpb):
        q = q_ref[:, hd * LANE:(hd + 1) * LANE]
        zero = jnp.zeros_like(q)
        q_s[2 * hd] = jnp.where(lane < DA_QK, q, zero)
        q_s[2 * hd + 1] = jnp.where(lane >= DA_QK, q, zero)
    m_s[...] = jnp.full_like(m_s, -jnp.inf)
    l_s[...] = jnp.zeros_like(l_s)
    acc_s[...] = jnp.zeros_like(acc_s)
    tr = tq // nsub
    chains = [(2 * hd + c, hd, pl.ds(r * tr, tr), r) for hd in range(hpb) for r in range(nsub) for c in range(2)]

    def step(j, masked):
        rows = pl.ds(pl.multiple_of(j * tq, tq), tq)
        k = [k_ref[rows, hd * LANE:(hd + 1) * LANE] for hd in range(hpb)]
        v = [v_ref[rows, hd * 2 * LANE:(hd + 1) * 2 * LANE] for hd in range(hpb)]
        nk = [(r + 1) * tr if masked else tq for _, _, _, r in chains]
        s = [lax.dot_general(q_s[n, rs, :], k[hd][:w], (((1,), (1,)), ((), ())), preferred_element_type=F32)
             for (n, hd, rs, _), w in zip(chains, nk)]
        if masked:
            s = [jnp.where(lax.broadcasted_iota(jnp.int32, (tr, w), 1) // CHUNK
                           <= (lax.broadcasted_iota(jnp.int32, (tr, w), 0) + ch[3] * tr) // CHUNK, x, -jnp.inf)
                 for x, ch, w in zip(s, chains, nk)]
        m_prev = [m_s[n, rs, :] for n, _, rs, _ in chains]
        m_new = [jnp.maximum(mp, jnp.max(x, axis=-1, keepdims=True)) for mp, x in zip(m_prev, s)]
        p = [jnp.exp2(x - jnp.concatenate([mn] * (w // LANE), axis=1)) for x, mn, w in zip(s, m_new, nk)]
        alpha = [jnp.exp2(mp - mn) for mp, mn in zip(m_prev, m_new)]
        pv = [jnp.dot(x.astype(MXU_DTYPE), v[ch[1]][:w], preferred_element_type=F32)
              for x, ch, w in zip(p, chains, nk)]
        for (n, _, rs, _), mn, a, o in zip(chains, m_new, alpha, pv):
            m_s[n, rs, :] = mn
            l_s[n, rs, :] = a * l_s[n, rs, :] + o[:, DA_V:]
            acc_s[n, rs, :] = a * acc_s[n, rs, :] + o[:, :DA_V]

    def block_group(i, carry):
        for u in range(kv_unroll):
            step(kv_unroll * i + u, False)
        return carry

    def block_single(j, carry):
        step(j, False)
        return carry

    n_group = qi // kv_unroll
    lax.fori_loop(0, n_group, block_group, 0)
    lax.fori_loop(n_group * kv_unroll, qi, block_single, 0)
    step(qi, True)
    lam = lam_ref[0, 0]
    for hd in range(hpb):
        o = acc_s[2 * hd] / l_s[2 * hd] - lam * (acc_s[2 * hd + 1] / l_s[2 * hd + 1])
        o = _rmsnorm_rows(o, g_ref[...]) * out_scale
        o_ref[:, hd * DA_V:(hd + 1) * DA_V] = o.astype(o_ref.dtype)


def _diff_attention(lam, pda, da_norm, B, S, lam_init, *, tq=512, nsub=2, hpb=2, kv_unroll=4):
    T = pda.shape[0]
    tq = min(tq, S)
    nq = S // tq
    HB = DA_HEADS // hpb
    kernel = functools.partial(_da_kernel, tq=tq, nsub=nsub, hpb=hpb, kv_unroll=kv_unroll,
                               out_scale=1.0 - lam_init)
    return pl.pallas_call(
        kernel,
        grid=(B, HB, nq),
        in_specs=[
            pl.BlockSpec(memory_space=pltpu.SMEM),
            pl.BlockSpec((tq, hpb * LANE), lambda b, h, i: (b * nq + i, h)),
            pl.BlockSpec((S, hpb * LANE), lambda b, h, i: (b, HB + h)),
            pl.BlockSpec((S, hpb * 2 * LANE), lambda b, h, i: (b, HB + h)),
            pl.BlockSpec((1, LANE), lambda b, h, i: (0, 0)),
        ],
        out_specs=pl.BlockSpec((tq, hpb * DA_V), lambda b, h, i: (b * nq + i, h)),
        out_shape=jax.ShapeDtypeStruct((T, DA_HEADS * DA_V), MXU_DTYPE),
        scratch_shapes=[
            pltpu.VMEM((2 * hpb, tq, LANE), MXU_DTYPE),
            pltpu.VMEM((2 * hpb, tq, LANE), F32),
            pltpu.VMEM((2 * hpb, tq, LANE), F32),
            pltpu.VMEM((2 * hpb, tq, DA_V), F32),
        ],
        compiler_params=_cparams(("parallel", "parallel", "arbitrary")),
        name="diff_attn",
    )(lam, pda, pda, pda, da_norm.reshape(1, DA_V))


def _gla_kernel(p_ref, w2_ref, gb_ref, ng_ref, o_ref, st_s, lg_s, *, tc, nb, ahead):
    i = pl.program_id(0)
    HK = GLA_HEADS * GLA_DK
    HV = GLA_HEADS * GLA_DV

    @pl.when(i == 0)
    def _():
        st_s[...] = jnp.zeros_like(st_s)

    for b in range(nb):
        z = _mm(p_ref[b, :, 2 * HK + 2 * HV:], w2_ref[...]) + gb_ref[...]
        lg_s[b] = (jnp.minimum(z, 0.0) - jnp.log1p(jnp.exp(-jnp.abs(z)))) * (1.0 / GLA_TAU)

    tri = _tri(CHUNK).astype(F32)
    causal = _tri(CHUNK)
    lane_k = lax.broadcasted_iota(jnp.int32, (CHUNK, HK), 1) // GLA_DK
    bd = (lax.broadcasted_iota(jnp.int32, (HV, HK), 0) // GLA_DV
          == lax.broadcasted_iota(jnp.int32, (HV, HK), 1) // GLA_DK)
    ng = ng_ref[...]

    def body(c, carry):
        inst = [(b, pl.ds(pl.multiple_of((c * ahead + u) * CHUNK, CHUNK), CHUNK))
                for u in range(ahead) for b in range(nb)]
        q = [p_ref[b, rows, 0:HK] * (GLA_DK ** -0.5) for b, rows in inst]
        k = [p_ref[b, rows, HK:2 * HK] for b, rows in inst]
        v = [p_ref[b, rows, 2 * HK:2 * HK + HV].astype(MXU_DTYPE) for b, rows in inst]
        G = [_sel_l(tri, lg_s[b, rows, :]) for b, rows in inst]
        g_mid = [x[CHUNK // 2:CHUNK // 2 + 1] for x in G]
        g_last = [x[CHUNK - 1:CHUNK] for x in G]
        qe = [x * jnp.exp(g - gm) for x, g, gm in zip(q, G, g_mid)]
        ke = [(x * jnp.exp(gm - g)).astype(MXU_DTYPE) for x, g, gm in zip(k, G, g_mid)]
        q_in = [x * jnp.exp(g) for x, g in zip(q, G)]
        k_end = [x * jnp.exp(gl - g) for x, g, gl in zip(k, G, g_last)]
        att = [[jnp.where(causal, _mm_nt(jnp.where(lane_k == h, x, 0.0), y), 0.0) for x, y in zip(qe, ke)]
               for h in range(GLA_HEADS)]
        y_intra = [[_mm(a, x[:, h * GLA_DV:(h + 1) * GLA_DV]) for a, x in zip(att[h], v)]
                   for h in range(GLA_HEADS)]
        kv = [jnp.where(bd, _mm_tn(x, y), 0.0) for x, y in zip(v, k_end)]
        for n, (b, rows) in enumerate(inst):
            st = st_s[b]
            y_inter = _mm_nt(q_in[n], st)
            st_s[b] = st * jnp.exp(g_last[n]) + kv[n]
            outs = [_rmsnorm_rows(y_intra[h][n] + y_inter[:, h * GLA_DV:(h + 1) * GLA_DV], ng)
                    for h in range(GLA_HEADS)]
            og = p_ref[b, rows, 2 * HK + HV:2 * HK + 2 * HV]
            o_ref[b, rows, :] = (jnp.concatenate(outs, axis=1) * _silu(og)).astype(o_ref.dtype)
        return carry

    lax.fori_loop(0, tc // (CHUNK * ahead), body, 0)


def _gla(pg, w2pad, gate_b, gla_norm, B, S, *, tc=512, ahead=4):
    T, N = pg.shape
    tc = min(tc, S)
    ns = S // tc
    HK, HV = GLA_HEADS * GLA_DK, GLA_HEADS * GLA_DV
    const = lambda i: (0, 0)
    out = pl.pallas_call(
        functools.partial(_gla_kernel, tc=tc, nb=B, ahead=ahead),
        grid=(ns,),
        in_specs=[
            pl.BlockSpec((B, tc, N), lambda i: (0, i, 0)),
            pl.BlockSpec((LANE, HK), const),
            pl.BlockSpec((1, HK), const),
            pl.BlockSpec((1, GLA_DV), const),
        ],
        out_specs=pl.BlockSpec((B, tc, HV), lambda i: (0, i, 0)),
        out_shape=jax.ShapeDtypeStruct((B, S, HV), MXU_DTYPE),
        scratch_shapes=[pltpu.VMEM((B, HV, HK), F32), pltpu.VMEM((B, tc, HK), F32)],
        compiler_params=_cparams(("arbitrary",)),
        name="gla",
    )(pg.reshape(B, S, N), w2pad, gate_b.reshape(1, HK), gla_norm.reshape(1, GLA_DV))
    return out.reshape(T, HV)


def _ssd_kernel(p_ref, cw_ref, cb_ref, dtb_ref, a_ref, e64_ref, e128_ref, dx_ref, ng_ref, o_ref,
                st_s, xpad_s, xs_s, bc_s, ax_s, ax2_s, xdt_s, *, tc, nb, ahead):
    i = pl.program_id(0)
    NI = SSD_INNER
    GN = SSD_GROUPS * SSD_N
    CD = NI + 2 * GN
    PAD = SUBLANE

    @pl.when(i == 0)
    def _():
        st_s[...] = jnp.zeros_like(st_s)
        xpad_s[:, 0:PAD, :] = jnp.zeros((nb, PAD, CD), F32)

    for b in range(nb):
        xpad_s[b, PAD:PAD + tc, :] = p_ref[b, :, NI:NI + CD]
        conv = cb_ref[...]
        for kk in range(SSD_CONV):
            off = PAD - (SSD_CONV - 1) + kk
            conv = conv + cw_ref[kk:kk + 1, :] * xpad_s[b, off:off + tc, :]
        xpad_s[b, 0:PAD, :] = xpad_s[b, tc:tc + PAD, :]
        xbc = _silu(conv)
        xs_s[b] = xbc[:, 0:NI]
        bc_s[b] = xbc[:, NI:]
        dt = _softplus(p_ref[b, :, NI + CD:] + dtb_ref[...])
        dta = dt * a_ref[...]
        ax_s[b] = _sel_r(dta, e64_ref[...])
        ax2_s[b] = _sel_r(dta, e128_ref[...])
        xdt_s[b] = xbc[:, 0:NI] * _sel_r(dt, e64_ref[...])

    tri = _tri(CHUNK).astype(F32)
    tri_t = (lax.broadcasted_iota(jnp.int32, (CHUNK, CHUNK), 0)
             <= lax.broadcasted_iota(jnp.int32, (CHUNK, CHUNK), 1)).astype(F32)
    causal = _tri(CHUNK)
    low_half = lax.broadcasted_iota(jnp.int32, (CHUNK, LANE), 1) < SSD_P
    rep = SSD_HEADS // SSD_GROUPS
    GW = rep * SSD_P

    def body(c, carry):
        inst = [(b, pl.ds(pl.multiple_of((c * ahead + u) * CHUNK, CHUNK), CHUNK))
                for u in range(ahead) for b in range(nb)]
        xdt = [xdt_s[b, rows, :] for b, rows in inst]
        acx = [_sel_l(tri, ax_s[b, rows, :]) for b, rows in inst]
        acx2 = [_sel_l(tri, ax2_s[b, rows, :]) for b, rows in inst]
        dta2 = [ax2_s[b, rows, :] for b, rows in inst]
        a_last = [x[CHUNK - 1:CHUNK] for x in acx]
        x_end = [(x * jnp.exp(al - a)).astype(MXU_DTYPE) for x, a, al in zip(xdt, acx, a_last)]
        xdt_m = [x.astype(MXU_DTYPE) for x in xdt]
        bm = [[bc_s[b, rows, g * SSD_N:(g + 1) * SSD_N].astype(MXU_DTYPE) for b, rows in inst]
              for g in range(SSD_GROUPS)]
        cm = [[bc_s[b, rows, GN + g * SSD_N:GN + (g + 1) * SSD_N].astype(MXU_DTYPE) for b, rows in inst]
              for g in range(SSD_GROUPS)]
        cb = [[_mm_nt(x, y) for x, y in zip(cm[g], bm[g])] for g in range(SSD_GROUPS)]
        new_st = [jnp.concatenate([_mm_tn(bm[g][n], x_end[n][:, g * GW:(g + 1) * GW])
                                   for g in range(SSD_GROUPS)], axis=1) for n in range(len(inst))]
        y_intra = []
        for n in range(len(inst)):
            parts = []
            for pair in range(SSD_HEADS // 2):
                res = []
                for hh in range(2):
                    h = pair * 2 + hh
                    col = acx2[n][:, h * LANE:h * LANE + CHUNK]
                    row = jnp.sum(tri_t * dta2[n][:, h * LANE:h * LANE + CHUNK], axis=0, keepdims=True)
                    lmat = jnp.exp(jnp.where(causal, col - row, -jnp.inf))
                    res.append(_mm(cb[h // rep][n] * lmat, xdt_m[n][:, pair * LANE:(pair + 1) * LANE]))
                parts.append(jnp.where(low_half, res[0], res[1]))
            y_intra.append(jnp.concatenate(parts, axis=1))
        half = NI // SSD_GROUPS
        for n, (b, rows) in enumerate(inst):
            st = st_s[b]
            inter = jnp.concatenate([_mm(cm[g][n], st[:, g * GW:(g + 1) * GW]) for g in range(SSD_GROUPS)],
                                    axis=1)
            st_s[b] = st * jnp.exp(a_last[n]) + new_st[n]
            y = y_intra[n] + inter * jnp.exp(acx[n]) + dx_ref[...] * xs_s[b, rows, :]
            y = y * _silu(p_ref[b, rows, 0:NI])
            outs = [_rmsnorm_rows(y[:, g * half:(g + 1) * half], ng_ref[:, g * half:(g + 1) * half])
                    for g in range(SSD_GROUPS)]
            o_ref[b, rows, :] = jnp.concatenate(outs, axis=1).astype(o_ref.dtype)
        return carry

    lax.fori_loop(0, tc // (CHUNK * ahead), body, 0)


def _ssd(ps, conv_w, conv_b, dtb_pad, a_pad, e64, e128, d_x, ssd_norm, B, S, *, tc=512, ahead=2):
    T, N = ps.shape
    tc = min(tc, S)
    ns = S // tc
    NI = SSD_INNER
    CD = NI + 2 * SSD_GROUPS * SSD_N
    const = lambda i: (0, 0)
    out = pl.pallas_call(
        functools.partial(_ssd_kernel, tc=tc, nb=B, ahead=ahead),
        grid=(ns,),
        in_specs=[
            pl.BlockSpec((B, tc, N), lambda i: (0, i, 0)),
            pl.BlockSpec((SSD_CONV, CD), const),
            pl.BlockSpec((1, CD), const),
            pl.BlockSpec((1, LANE), const),
            pl.BlockSpec((1, LANE), const),
            pl.BlockSpec((LANE, NI), const),
            pl.BlockSpec((LANE, SSD_HEADS * LANE), const),
            pl.BlockSpec((1, NI), const),
            pl.BlockSpec((1, NI), const),
        ],
        out_specs=pl.BlockSpec((B, tc, NI), lambda i: (0, i, 0)),
        out_shape=jax.ShapeDtypeStruct((B, S, NI), MXU_DTYPE),
        scratch_shapes=[
            pltpu.VMEM((B, SSD_N, NI), F32),
            pltpu.VMEM((B, tc + SUBLANE, CD), F32),
            pltpu.VMEM((B, tc, NI), F32),
            pltpu.VMEM((B, tc, 2 * SSD_GROUPS * SSD_N), F32),
            pltpu.VMEM((B, tc, NI), F32),
            pltpu.VMEM((B, tc, SSD_HEADS * LANE), F32),
            pltpu.VMEM((B, tc, NI), F32),
        ],
        compiler_params=_cparams(("arbitrary",)),
        name="ssd",
    )(ps.reshape(B, S, N), conv_w, conv_b.reshape(1, CD), dtb_pad, a_pad, e64, e128, d_x,
      ssd_norm.reshape(1, NI))
    return out.reshape(T, NI)


def _unit_lower_inverse(mats, eye, same_blk, stack):
    d = [jnp.where(same_blk, a, 0.0) for a in mats]
    o = [a - x for a, x in zip(mats, d)]
    t = [eye + x for x in d]
    dp = d
    sd = [stack(x) for x in dp]
    for _ in range(3):
        dp = [_mm(x, y) for x, y in zip(dp, sd)]
        sd = [stack(x) for x in dp]
        t = [x + _mm(x, y) for x, y in zip(t, sd)]
    n = [_mm(x, stack(y)) for x, y in zip(t, o)]
    n2 = [_mm(x, stack(x)) for x in n]
    q = [eye + x for x in n]
    q = [x + _mm(x, stack(y)) for x, y in zip(q, n2)]
    return [_mm(x, stack(y)) for x, y in zip(q, t)]


def _rw_kernel(p_ref, mu_ref, w0_ref, w2_ref, a0_ref, a2_ref, g2_ref, kk_ref, ka_ref, rk_ref,
               nw_ref, nb_ref, bd_ref, o_ref,
               mt_s, ppad_s, r_s, k_s, v_s, lw_s, al_s, be_s, g_s, *, tc, nb, ahead):
    i = pl.program_id(0)
    R = RW_DIM
    PAD = SUBLANE
    NP = p_ref.shape[2]
    NPAIR = RW_HEADS // 2

    @pl.when(i == 0)
    def _():
        mt_s[...] = jnp.zeros_like(mt_s)
        ppad_s[:, 0:PAD, :] = jnp.zeros((nb, PAD, NP), F32)

    bd = bd_ref[...]
    for b in range(nb):
        p = p_ref[b]
        ppad_s[b, PAD:PAD + tc, :] = p
        prev = ppad_s[b, PAD - 1:PAD - 1 + tc, :]
        ppad_s[b, 0:PAD, :] = ppad_s[b, tc:tc + PAD, :]
        pm = p + (prev - p) * mu_ref[...]
        r = pm[:, 0:R]
        k = pm[:, R:2 * R]
        v = pm[:, 2 * R:3 * R]
        wa = pm[:, 3 * R:3 * R + LANE]
        gl = pm[:, 3 * R + LANE:3 * R + 2 * LANE]
        lw_s[b] = -RW_DECAY_SCALE * _sigmoid(w0_ref[...] + _mm(jnp.tanh(wa), w2_ref[...]))
        a = _sigmoid(a0_ref[...] + _mm(wa, a2_ref[...]))
        g_s[b] = _mm(_sigmoid(gl), g2_ref[...])
        kk = k * kk_ref[...]
        kk = kk / jnp.maximum(jnp.sqrt(_sel_r(kk * kk, bd)), 1e-12)
        r_s[b] = r
        k_s[b] = k * (1.0 + (a - 1.0) * ka_ref[...])
        v_s[b] = v
        al_s[b] = -kk
        be_s[b] = kk * a

    C2 = 2 * CHUNK
    tri = _tri(CHUNK).astype(F32)
    tt = lax.broadcasted_iota(jnp.int32, (CHUNK, C2), 0)
    ss = lax.broadcasted_iota(jnp.int32, (CHUNK, C2), 1) % CHUNK
    strict = ss < tt
    incl = ss <= tt
    eye = (ss == tt).astype(F32)
    same_blk = (tt // 16) == (ss // 16)
    m0 = lax.broadcasted_iota(jnp.int32, (CHUNK, LANE), 1) < RW_HEAD
    bdm = (lax.broadcasted_iota(jnp.int32, (C2, C2), 0) // CHUNK
           == lax.broadcasted_iota(jnp.int32, (C2, C2), 1) // CHUNK)

    def stack(x):
        xb = x.astype(MXU_DTYPE)
        zero = jnp.zeros_like(xb)
        return jnp.concatenate([jnp.where(m0, xb, zero), jnp.where(m0, zero, xb)], axis=0)

    def body(c, carry):
        row_sl = [pl.ds(pl.multiple_of((c * ahead + u) * CHUNK, CHUNK), CHUNK) for u in range(ahead)]
        w_end, vv_b, wide = [], [], []
        for rows in row_sl:
            for b in range(nb):
                lw = lw_s[b, rows, :]
                cum = _sel_l(tri, lw)
                e_cum = jnp.exp(cum)
                e_inv = jnp.exp(-cum)
                ah = al_s[b, rows, :] * jnp.exp(cum - lw)
                bh = be_s[b, rows, :] * e_inv
                kh = k_s[b, rows, :] * e_inv
                rh = r_s[b, rows, :] * e_cum
                vv = v_s[b, rows, :]
                vv_b.append(vv)
                for j in range(NPAIR):
                    sl = slice(j * LANE, (j + 1) * LANE)
                    w_end.append(e_cum[CHUNK - 1:CHUNK, sl])
                    wide.append(tuple(x[:, sl].astype(MXU_DTYPE) for x in (ah, rh, bh, kh, vv)))
        nch = nb * NPAIR
        ar = [jnp.concatenate([w[0], w[1]], axis=0) for w in wide]
        bk = [jnp.concatenate([stack(w[2]), stack(w[3])], axis=0) for w in wide]
        v_bd = [stack(w[4]) for w in wide]
        big = [_mm_nt(x, y) for x, y in zip(ar, bk)]
        a_ab = [jnp.where(strict, x[0:CHUNK, 0:C2], 0.0) for x in big]
        a_ak = [jnp.where(strict, x[0:CHUNK, C2:], 0.0) for x in big]
        r_b = [jnp.where(incl, x[CHUNK:, 0:C2], 0.0) for x in big]
        r_k = [jnp.where(incl, x[CHUNK:, C2:], 0.0) for x in big]
        t_inv = _unit_lower_inverse(a_ab, eye, same_blk, stack)
        for u, rows in enumerate(row_sl):
            ch = slice(u * nch, (u + 1) * nch)
            mt = [mt_s[n] for n in range(nch)]
            rhs = [_mm_nt(w[0], m) + _mm(x, v) for w, m, x, v in zip(wide[ch], mt, a_ak[ch], v_bd[ch])]
            u_w = [_mm(x, stack(y)) for x, y in zip(t_inv[ch], rhs)]
            upd = [_mm_tn(jnp.concatenate([x.astype(MXU_DTYPE), w[4]], axis=0),
                          jnp.concatenate([w[2], w[3]], axis=0)) for x, w in zip(u_w, wide[ch])]
            for n in range(nch):
                mt_s[n] = (mt[n] + jnp.where(bdm, upd[n], 0.0)) * w_end[u * nch + n]
            y_w = [_mm_nt(w[1], m) + _mm(x, stack(uu)) + _mm(z, v)
                   for w, m, x, uu, z, v in zip(wide[ch], mt, r_b[ch], u_w, r_k[ch], v_bd[ch])]
            for b in range(nb):
                y = jnp.concatenate([y_w[b * NPAIR + j] for j in range(NPAIR)], axis=1)
                mean = _mm(y, bd) * (1.0 / RW_HEAD)
                yc = y - mean
                var = _mm(yc * yc, bd) * (1.0 / RW_HEAD)
                yn = yc * lax.rsqrt(var + RW_GN_EPS) * nw_ref[...] + nb_ref[...]
                bonus = _mm(r_s[b, rows, :] * k_s[b, rows, :] * rk_ref[...], bd) * vv_b[u * nb + b]
                o_ref[b, rows, :] = ((yn + bonus) * g_s[b, rows, :]).astype(o_ref.dtype)
        return carry

    lax.fori_loop(0, tc // (CHUNK * ahead), body, 0)


def _rwkv(pr, mu, w0, w2pad, a0, a2pad, g2, k_k, k_a, r_k, norm_w, norm_b, bd, B, S, *, tc=256, ahead=4):
    T, N = pr.shape
    tc = min(tc, S)
    ns = S // tc
    R = RW_DIM
    const = lambda i: (0, 0)
    row = lambda x: x.reshape(1, -1)
    vec = pl.BlockSpec((1, R), const)
    out = pl.pallas_call(
        functools.partial(_rw_kernel, tc=tc, nb=B, ahead=ahead),
        grid=(ns,),
        in_specs=[
            pl.BlockSpec((B, tc, N), lambda i: (0, i, 0)),
            pl.BlockSpec((1, N), const),
            vec,
            pl.BlockSpec((LANE, R), const),
            vec,
            pl.BlockSpec((LANE, R), const),
            pl.BlockSpec((LANE, R), const),
            vec, vec, vec, vec, vec,
            pl.BlockSpec((R, R), const),
        ],
        out_specs=pl.BlockSpec((B, tc, R), lambda i: (0, i, 0)),
        out_shape=jax.ShapeDtypeStruct((B, S, R), MXU_DTYPE),
        scratch_shapes=[
            pltpu.VMEM((B * RW_HEADS // 2, LANE, LANE), F32),
            pltpu.VMEM((B, tc + SUBLANE, N), F32),
        ] + [pltpu.VMEM((B, tc, R), F32) for _ in range(7)],
        compiler_params=_cparams(("arbitrary",)),
        name="rwkv7",
    )(pr.reshape(B, S, N), row(mu), row(w0), w2pad, row(a0), a2pad, g2, row(k_k), row(k_a), row(r_k),
      row(norm_w), row(norm_b), bd)
    return out.reshape(T, R)


def _merge_kernel(h_ref, g_ref, wgate_ref, gb_ref, ya_ref, yb_ref, yc_ref, yd_ref, wb_ref, wo_ref, o_ref):
    h = h_ref[...]
    D = h.shape[1]
    xn = _rmsnorm_rows(h, g_ref[...]).astype(wgate_ref.dtype)
    merged = jnp.zeros(h.shape, F32)
    for n, y_ref in enumerate((ya_ref, yb_ref, yc_ref, yd_ref)):
        logits = jnp.dot(xn, wgate_ref[:, n * D:(n + 1) * D], preferred_element_type=F32) + gb_ref[n:n + 1, :]
        proj = jnp.dot(y_ref[...], wb_ref[n], preferred_element_type=F32)
        merged = merged + _sigmoid(logits) * proj
    o_ref[...] = h + jnp.dot(merged.astype(wo_ref.dtype), wo_ref[...], preferred_element_type=F32)


def _merge(h, g, wgate, gate_b, ys, w_branch, w_out, l, *, tm=512):
    T, D = h.shape
    tm = min(tm, T)
    _, NB, MW, _ = w_branch.shape
    const2 = lambda i: (0, 0)
    ytile = pl.BlockSpec((tm, MW), lambda i: (i, 0))
    return pl.pallas_call(
        _merge_kernel,
        grid=(T // tm,),
        in_specs=[
            pl.BlockSpec((tm, D), lambda i: (i, 0)),
            pl.BlockSpec((1, D), const2),
            _layer_block(wgate, l),
            pl.BlockSpec((NB, D), const2),
            ytile, ytile, ytile, ytile,
            _layer_block(w_branch, l),
            _layer_block(w_out, l),
        ],
        out_specs=pl.BlockSpec((tm, D), lambda i: (i, 0)),
        out_shape=jax.ShapeDtypeStruct((T, D), F32),
        compiler_params=_cparams(("parallel",)),
        name="merge",
    )(h, g.reshape(1, D), wgate, gate_b, *ys, w_branch, w_out)


def _pad_cols(w, n):
    return jnp.pad(w, ((0, 0), (0, n - w.shape[1])))


def _pad_rows_at(w, start, total):
    return jnp.pad(w, ((start, total - start - w.shape[0]), (0, 0)))


def _rope_tables(S):
    half = DA_QK // 2
    inv_freq = ROPE_THETA ** (-jnp.arange(half, dtype=F32) / half)
    ang = jnp.arange(S, dtype=F32)[:, None] * inv_freq[None, :]
    cos, sin = jnp.cos(ang), jnp.sin(ang)
    reps = LANE // DA_QK
    return jnp.tile(jnp.concatenate([cos, cos], axis=1), (1, reps)), jnp.tile(jnp.concatenate([-sin, sin], axis=1), (1, reps))


def _head_expand(heads, width):
    r = jnp.arange(LANE)[:, None]
    c = jnp.arange(heads * width)[None, :] // width
    return (r == c).astype(F32)


def kernel(x, ffn1_norm, ffn1_wg, ffn1_wu, ffn1_wd, mix_norm, w_in, da_lambda_q1, da_lambda_k1, da_lambda_q2, da_lambda_k2, da_norm, gla_gate_w2, gla_gate_b, gla_norm, ssd_conv_w, ssd_conv_b, ssd_dt_bias, ssd_a_log, ssd_d, ssd_norm, rw_mu, rw_w0, rw_w2, rw_a0, rw_a2, rw_g2, rw_k_k, rw_k_a, rw_r_k, rw_norm_w, rw_norm_b, w_branch, gate_b, w_out, ffn2_norm, ffn2_wg, ffn2_wu, ffn2_wd, final_norm):
    B, S, D = x.shape
    depth = w_in.shape[0]
    T = B * S
    h = x.reshape(T, D)
    cos_t, sin_t = _rope_tables(S)
    e64 = _head_expand(SSD_HEADS, SSD_P)
    e128 = _head_expand(SSD_HEADS, LANE)
    rw_bd = (jnp.arange(RW_DIM)[:, None] // RW_HEAD == jnp.arange(RW_DIM)[None, :] // RW_HEAD).astype(F32)

    n_da = DA_HEADS * (4 * DA_QK + DA_V)
    n_gla = 2 * GLA_HEADS * GLA_DK + 2 * GLA_HEADS * GLA_DV + GLA_RANK
    n_ssd = SSD_INNER + (SSD_INNER + 2 * SSD_GROUPS * SSD_N) + SSD_HEADS
    n_rw = 3 * RW_DIM + rw_w2.shape[1] + rw_a2.shape[1] + rw_g2.shape[1]
    o_gla, o_ssd, o_rw, o_gate = n_da, n_da + n_gla, n_da + n_gla + n_ssd, n_da + n_gla + n_ssd + n_rw
    cast = lambda w: w.astype(MXU_DTYPE)
    pad_last = lambda w: jnp.pad(w, ((0, 0), (0, 0), (0, -w.shape[2] % LANE)))
    f1g, f1u, f1d = cast(ffn1_wg), cast(ffn1_wu), cast(ffn1_wd)
    f2g, f2u, f2d = cast(ffn2_wg), cast(ffn2_wu), cast(ffn2_wd)
    w_da = cast(w_in[:, :, 0:n_da])
    w_gla = cast(pad_last(w_in[:, :, o_gla:o_ssd]))
    w_ssd = cast(pad_last(w_in[:, :, o_ssd:o_rw]))
    w_rw = cast(w_in[:, :, o_rw:o_gate])
    w_gate = cast(w_in[:, :, o_gate:])
    wb, wo = cast(w_branch), cast(w_out)

    for l in range(depth):
        h = _ffn(h, ffn1_norm[l], f1g, f1u, f1d, final_norm, l, final=False)
        g = mix_norm[l]

        lam_init = 0.8 - 0.6 * math.exp(-0.3 * l)
        lam = (jnp.exp(jnp.sum(da_lambda_q1[l] * da_lambda_k1[l])) - jnp.exp(jnp.sum(da_lambda_q2[l] * da_lambda_k2[l]))
               + lam_init).reshape(1, 1).astype(F32)
        pda = _proj_da(h, g, w_da, l, cos_t, sin_t, S)
        y_a = _diff_attention(lam, pda, da_norm[l], B, S, lam_init)

        pg = _proj(h, g, w_gla, l, F32)
        y_b = _gla(pg, _pad_rows_at(gla_gate_w2[l], 0, LANE), gla_gate_b[l], gla_norm[l], B, S)

        ps = _proj(h, g, w_ssd, l, F32)
        dtb_pad = _pad_cols(ssd_dt_bias[l].reshape(1, -1), LANE)
        a_pad = _pad_cols(-jnp.exp(ssd_a_log[l]).reshape(1, -1), LANE)
        d_x = jnp.repeat(ssd_d[l], SSD_P).reshape(1, -1)
        y_c = _ssd(ps, ssd_conv_w[l], ssd_conv_b[l], dtb_pad, a_pad, e64, e128, d_x, ssd_norm[l], B, S)

        pr = _proj(h, g, w_rw, l, F32)
        rank_w = rw_w2.shape[1]
        y_d = _rwkv(pr, rw_mu[l], rw_w0[l], _pad_rows_at(rw_w2[l], 0, LANE), rw_a0[l],
                    _pad_rows_at(rw_a2[l], rank_w, LANE), cast(rw_g2[l]), rw_k_k[l], rw_k_a[l], rw_r_k[l],
                    rw_norm_w[l], rw_norm_b[l], rw_bd, B, S)

        h = _merge(h, g, w_gate, gate_b[l], (y_a, y_b, y_c, y_d), wb, wo, l)
        h = _ffn(h, ffn2_norm[l], f2g, f2u, f2d, final_norm, l, final=(l == depth - 1))
    return h.reshape(B, S, D)
```
